```python
import jax, jax.numpy as jnp
from jax import lax
import numpy as np

D_MODEL = 1024
BATCH = 4
SEQ = 8192
DEPTH = 2

HEAD_DIM = 64
ROPE_THETA = 10000.0
EPS = 1e-6
NEG_INF = -1e30
Q_BLOCK = 128

MLA_HEADS = 8
MLA_Q_RANK = 256
MLA_KV_RANK = 128
MLA_NOPE_DIM = 64
MLA_ROPE_DIM = 32
MLA_V_DIM = 64
DIL_PATTERNS = ((128, 1), (512, 4), (2048, 16))
DIL_HEADS_PER_GROUP = 4
DIL_HEADS = DIL_HEADS_PER_GROUP * len(DIL_PATTERNS)
BAND_BLOCK = 128
NA_HEADS = 8
NA_KH = 8
NA_KW = 16
GRID_W = 64
N_BRANCH = 3
D_FF = -(-(8 * D_MODEL) // (3 * 256)) * 256

A_COLS = MLA_Q_RANK + MLA_KV_RANK + MLA_ROPE_DIM
B_COLS = 3 * DIL_HEADS * HEAD_DIM
C_COLS = 3 * NA_HEADS * HEAD_DIM
G_COLS = N_BRANCH * D_MODEL
IN_COLS = A_COLS + B_COLS + C_COLS + G_COLS
IN_SPLITS = (MLA_Q_RANK, MLA_Q_RANK + MLA_KV_RANK, A_COLS, A_COLS + B_COLS, A_COLS + B_COLS + C_COLS)
A_OUT = MLA_HEADS * MLA_V_DIM
B_OUT = DIL_HEADS_PER_GROUP * HEAD_DIM
C_OUT = NA_HEADS * HEAD_DIM

kernel_name = 'hybrid_mla_dilated_neighbourhood_encoder'


def rms_norm(x, g):
    xf = x.astype(jnp.float32)
    y = xf * lax.rsqrt(jnp.mean(xf * xf, axis=-1, keepdims=True) + EPS)
    return (y * g.astype(jnp.float32)).astype(x.dtype)


def rope_tables(seq_len, dim):
    pos = jnp.arange(seq_len, dtype=jnp.float32)
    inv = jnp.power(ROPE_THETA, -jnp.arange(0, dim, 2, dtype=jnp.float32) / dim)
    ang = pos[:, None] * inv[None, :]
    return jnp.cos(ang), jnp.sin(ang)


def apply_rope(x, cos, sin):
    xf = x.astype(jnp.float32)
    x1, x2 = jnp.split(xf, 2, axis=-1)
    return jnp.concatenate([x1 * cos - x2 * sin, x2 * cos + x1 * sin], axis=-1).astype(x.dtype)


def mla_mixer(c_q, c_kv, k_rope, g_q, g_kv, w_uq, w_ukv):
    B, S = c_q.shape[0], c_q.shape[1]
    q = (rms_norm(c_q, g_q) @ w_uq).reshape(B, S, MLA_HEADS, MLA_NOPE_DIM + MLA_ROPE_DIM)
    kv = (rms_norm(c_kv, g_kv) @ w_ukv).reshape(B, S, MLA_HEADS, MLA_NOPE_DIM + MLA_V_DIM)
    cos, sin = rope_tables(S, MLA_ROPE_DIM)
    q_nope = q[..., :MLA_NOPE_DIM]
    q_rope = apply_rope(q[..., MLA_NOPE_DIM:], cos[:, None], sin[:, None])
    k_nope, v = kv[..., :MLA_NOPE_DIM], kv[..., MLA_NOPE_DIM:]
    k_rope = apply_rope(k_rope, cos, sin)
    scale = (MLA_NOPE_DIM + MLA_ROPE_DIM) ** -0.5
    nb = S // Q_BLOCK

    def to_blocks(t):
        return t.reshape(B, nb, Q_BLOCK, *t.shape[2:]).swapaxes(0, 1)

    def attend(blk):
        qn, qr = blk
        s = (jnp.einsum('bqhd,bkhd->bhqk', qn, k_nope, preferred_element_type=jnp.float32)
             + jnp.einsum('bqhr,bkr->bhqk', qr, k_rope, preferred_element_type=jnp.float32)) * scale
        p = jax.nn.softmax(s, axis=-1).astype(v.dtype)
        return jnp.einsum('bhqk,bkhd->bqhd', p, v)

    o = lax.map(attend, (to_blocks(q_nope), to_blocks(q_rope)))
    return o.swapaxes(0, 1).reshape(B, S, A_OUT)


def banded_attention(q, k, v, radius):
    N, L, H, D = q.shape
    nb = -(-L // BAND_BLOCK)
    Lp = nb * BAND_BLOCK
    kw = BAND_BLOCK + 2 * radius
    qb = jnp.pad(q, ((0, 0), (0, Lp - L), (0, 0), (0, 0))).reshape(N, nb, BAND_BLOCK, H, D)
    pad_k = ((0, 0), (radius, Lp - L + radius), (0, 0), (0, 0))
    idx = jnp.arange(nb)[:, None] * BAND_BLOCK + jnp.arange(kw)[None, :]
    kb = jnp.pad(k, pad_k)[:, idx]
    vb = jnp.pad(v, pad_k)[:, idx]
    s = jnp.einsum('ncqhd,nckhd->nchqk', qb, kb, preferred_element_type=jnp.float32) * (D ** -0.5)
    qpos = jnp.arange(nb)[:, None] * BAND_BLOCK + jnp.arange(BAND_BLOCK)[None, :]
    kpos = (idx - radius)[:, None, :]
    mask = (jnp.abs(qpos[:, :, None] - kpos) <= radius) & (kpos >= 0) & (kpos < L)
    s = jnp.where(mask[None, :, None], s, NEG_INF)
    m = jnp.max(s, axis=-1, keepdims=True)
    e = jnp.exp(s - m)
    den = jnp.sum(e, axis=-1, keepdims=True)
    p = (e / den).astype(v.dtype)
    lse = (m + jnp.log(den))[..., 0]
    o = jnp.einsum('nchqk,nckhd->ncqhd', p, vb).reshape(N, Lp, H, D)[:, :L]
    lse = lse.transpose(0, 1, 3, 2).reshape(N, Lp, H)[:, :L]
    return o, lse


def dilated_mixer(q, k, v):
    B, S = q.shape[0], q.shape[1]
    H, D = DIL_HEADS_PER_GROUP, HEAD_DIM
    outs, lses = [], []
    for g, (window, dilation) in enumerate(DIL_PATTERNS):
        hs = slice(g * H, (g + 1) * H)
        L = S // dilation

        def split(t):
            return t[:, :, hs].reshape(B, L, dilation, H, D).swapaxes(1, 2).reshape(B * dilation, L, H, D)

        o, lse = banded_attention(split(q), split(k), split(v), window // (2 * dilation))
        outs.append(o.reshape(B, dilation, L, H, D).swapaxes(1, 2).reshape(B, S, H, D))
        lses.append(lse.reshape(B, dilation, L, H).swapaxes(1, 2).reshape(B, S, H))
    alpha = jax.nn.softmax(jnp.stack(lses), axis=0).astype(q.dtype)
    o = jnp.einsum('gbsh,gbshd->bshd', alpha, jnp.stack(outs))
    return o.reshape(B, S, B_OUT)


def neighbourhood_mixer(q, k, v, rpb):
    B, S, H, D = q.shape
    rows = S // GRID_W
    kh = min(NA_KH, rows)
    r = jnp.arange(rows)
    c = jnp.arange(GRID_W)
    rs = jnp.clip(r - kh // 2, 0, rows - kh)
    cs = jnp.clip(c - NA_KW // 2, 0, GRID_W - NA_KW)
    key_cols = cs[:, None] + jnp.arange(NA_KW)[None, :]
    dc = key_cols - c[:, None] + (NA_KW - 1)
    scale = D ** -0.5

    def attend_row(inp):
        q_row, r0, r_i = inp
        key_rows = r0 + jnp.arange(kh)
        idx = key_rows[None, :, None] * GRID_W + key_cols[:, None, :]
        kg = k[:, idx].reshape(B, GRID_W, kh * NA_KW, H, D)
        vg = v[:, idx].reshape(B, GRID_W, kh * NA_KW, H, D)
        dr = key_rows - r_i + (NA_KH - 1)
        bias = rpb[:, dr[None, :, None], dc[:, None, :]].reshape(H, GRID_W, kh * NA_KW)
        s = jnp.einsum('bchd,bcnhd->bhcn', q_row, kg, preferred_element_type=jnp.float32) * scale
        s = s + bias.astype(jnp.float32)[None]
        p = jax.nn.softmax(s, axis=-1).astype(v.dtype)
        return jnp.einsum('bhcn,bcnhd->bchd', p, vg)

    q_rows = q.reshape(B, rows, GRID_W, H, D).swapaxes(0, 1)
    o = lax.map(attend_row, (q_rows, rs, r))
    return o.swapaxes(0, 1).reshape(B, S, C_OUT)


def setup_inputs(seed: int = 0) -> dict:
    key = jax.random.key(seed)
    ks = jax.random.split(key, 17)

    def normal(k, shape, fan_in):
        return jax.random.normal(k, shape, jnp.float32) * (fan_in ** -0.5)

    def gain(k, shape):
        return 1.0 + 0.01 * jax.random.normal(k, shape, jnp.float32)

    return {
        'x': jax.random.normal(ks[0], (BATCH, SEQ, D_MODEL), jnp.float32),
        'w_in': normal(ks[1], (DEPTH, D_MODEL, IN_COLS), D_MODEL),
        'g_mix': gain(ks[2], (DEPTH, D_MODEL)),
        'g_q': gain(ks[3], (DEPTH, MLA_Q_RANK)),
        'g_kv': gain(ks[4], (DEPTH, MLA_KV_RANK)),
        'w_uq': normal(ks[5], (DEPTH, MLA_Q_RANK, MLA_HEADS * (MLA_NOPE_DIM + MLA_ROPE_DIM)), MLA_Q_RANK),
        'w_ukv': normal(ks[6], (DEPTH, MLA_KV_RANK, MLA_HEADS * (MLA_NOPE_DIM + MLA_V_DIM)), MLA_KV_RANK),
        'rpb': 0.1 * jax.random.normal(ks[7], (DEPTH, NA_HEADS, 2 * NA_KH - 1, 2 * NA_KW - 1), jnp.float32),
        'w_pa': normal(ks[8], (DEPTH, A_OUT, D_MODEL), A_OUT),
        'w_pb': normal(ks[9], (DEPTH, B_OUT, D_MODEL), B_OUT),
        'w_pc': normal(ks[10], (DEPTH, C_OUT, D_MODEL), C_OUT),
        'w_o': normal(ks[11], (DEPTH, D_MODEL, D_MODEL), D_MODEL),
        'g_ffn': gain(ks[12], (DEPTH, D_MODEL)),
        'w1': normal(ks[13], (DEPTH, D_MODEL, D_FF), D_MODEL),
        'w3': normal(ks[14], (DEPTH, D_MODEL, D_FF), D_MODEL),
        'w2': normal(ks[15], (DEPTH, D_FF, D_MODEL), D_FF),
        'g_final': gain(ks[16], (D_MODEL,)),
    }


def reference(x, w_in, g_mix, g_q, g_kv, w_uq, w_ukv, rpb, w_pa, w_pb, w_pc, w_o, g_ffn, w1, w3, w2, g_final):
    B, S = x.shape[0], x.shape[1]
    cos_b, sin_b = rope_tables(S, HEAD_DIM)
    for l in range(DEPTH):
        h = rms_norm(x, g_mix[l])
        proj = h @ w_in[l]
        c_q, c_kv, k_r, qkv_b, qkv_c, gate_logits = jnp.split(proj, IN_SPLITS, axis=-1)
        y_a = mla_mixer(c_q, c_kv, k_r, g_q[l], g_kv[l], w_uq[l], w_ukv[l])
        qkv_b = qkv_b.reshape(B, S, 3, DIL_HEADS, HEAD_DIM)
        q_b = apply_rope(qkv_b[:, :, 0], cos_b[:, None], sin_b[:, None])
        k_b = apply_rope(qkv_b[:, :, 1], cos_b[:, None], sin_b[:, None])
        y_b = dilated_mixer(q_b, k_b, qkv_b[:, :, 2])
        qkv_c = qkv_c.reshape(B, S, 3, NA_HEADS, HEAD_DIM)
        y_c = neighbourhood_mixer(qkv_c[:, :, 0], qkv_c[:, :, 1], qkv_c[:, :, 2], rpb[l])
        gates = jax.nn.sigmoid(gate_logits.astype(jnp.float32)).astype(x.dtype).reshape(B, S, N_BRANCH, D_MODEL)
        merged = (gates[:, :, 0] * (y_a @ w_pa[l])
                  + gates[:, :, 1] * (y_b @ w_pb[l])
                  + gates[:, :, 2] * (y_c @ w_pc[l]))
        x = x + merged @ w_o[l]
        h = rms_norm(x, g_ffn[l])
        x = x + (jax.nn.silu(h @ w1[l]) * (h @ w3[l])) @ w2[l]
    return rms_norm(x, g_final)
```

```python
import functools
import math

import jax
import jax.numpy as jnp
import numpy as np
from jax import lax
from jax.experimental import pallas as pl
from jax.experimental.pallas import tpu as pltpu

F32 = jnp.float32
BF16 = jnp.bfloat16

D_MODEL = 1024
HEAD_DIM = 64
ROPE_THETA = 10000.0
EPS = 1e-6
MASK_VALUE = -1e30
LOG2E = 1.4426950408889634

MLA_HEADS = 8
MLA_Q_RANK = 256
MLA_KV_RANK = 128
MLA_NOPE_DIM = 64
MLA_ROPE_DIM = 32
MLA_V_DIM = 64
MLA_HEAD_PAD = 128

DIL_PATTERNS = ((128, 1), (512, 4), (2048, 16))
DIL_HEADS_PER_GROUP = 4
DIL_HEADS = DIL_HEADS_PER_GROUP * len(DIL_PATTERNS)
DIL_WIDTH = DIL_HEADS * HEAD_DIM
DIL_GROUP_WIDTH = DIL_HEADS_PER_GROUP * HEAD_DIM

NA_HEADS = 8
NA_KH = 8
NA_KW = 16
GRID_W = 64
NA_WIDTH = NA_HEADS * HEAD_DIM
NA_ROWS_PER_BLOCK = 8
NA_WINDOW_ROWS = 16

N_BRANCH = 3
D_FF = -(-(8 * D_MODEL) // (3 * 256)) * 256

LANE = 128

COL_C = 0
COL_QB = 512
COL_KB = COL_QB + DIL_WIDTH
COL_VB = COL_KB + DIL_WIDTH
COL_QBR = COL_VB + DIL_WIDTH
COL_KBR = COL_QBR + DIL_WIDTH
COL_QKVC = COL_KBR + DIL_WIDTH
COL_G = COL_QKVC + 3 * NA_WIDTH
COL_END = COL_G + N_BRANCH * D_MODEL

VMEM_LIMIT = 56 * 1024 * 1024


def _dot(a, b):
    return jnp.dot(a, b, preferred_element_type=F32)


def _dot_nt(a, b):
    return lax.dot_general(a, b, (((1,), (1,)), ((), ())), preferred_element_type=F32)


def _dot_tn(a, b):
    return lax.dot_general(a, b, (((0,), (0,)), ((), ())), preferred_element_type=F32)


def _rms(x, g):
    return x * lax.rsqrt(jnp.mean(x * x, axis=-1, keepdims=True) + EPS) * g


def _resident(shape):
    del shape
    return pl.BlockSpec(memory_space=pltpu.VMEM)


def _in_proj_kernel(x_ref, g_ref, w_ref, gq_ref, gkv_ref, wqT_ref, wqrT_ref, wk_ref, wvT_ref,
                    cqT_ref, sqT_ref, tk_ref, cbq_ref, sbq_ref, cbk_ref, sbk_ref,
                    qT_out, k_out, vT_out, qb_out, kb_out, vb_out, qkvc_out, gates_out):
    h = _rms(x_ref[...], g_ref[...]).astype(BF16)

    def proj(lo, hi):
        return _dot(h, w_ref[:, lo:hi])

    c = proj(COL_C, COL_QB)
    cqn = _rms(c[:, :MLA_Q_RANK], gq_ref[...]).astype(BF16)
    ckvn = _rms(c[:, MLA_Q_RANK:MLA_Q_RANK + MLA_KV_RANK], gkv_ref[...]).astype(BF16)
    kr = (c[:, MLA_Q_RANK + MLA_KV_RANK:] * tk_ref[...]).astype(BF16)
    k_out[...] = _dot(jnp.concatenate([ckvn, kr], axis=1), wk_ref[...]).astype(BF16)
    vT_out[0] = _dot_nt(wvT_ref[...], ckvn).astype(BF16)
    q_main = _dot_nt(wqT_ref[...], cqn)
    q_rot = _dot_nt(wqrT_ref[...], cqn)
    cq = cqT_ref[...]
    sq = sqT_ref[...]
    for hd in range(MLA_HEADS):
        rows = slice(hd * MLA_HEAD_PAD, (hd + 1) * MLA_HEAD_PAD)
        qT_out[0, rows, :] = (q_main[rows] * cq + q_rot[rows] * sq).astype(BF16)

    reps = DIL_WIDTH // LANE
    qb = proj(COL_QB, COL_KB) * jnp.tile(cbq_ref[...], (1, reps)) \
        + proj(COL_QBR, COL_KBR) * jnp.tile(sbq_ref[...], (1, reps))
    qb_out[...] = qb.astype(BF16)
    kb = proj(COL_KB, COL_VB) * jnp.tile(cbk_ref[...], (1, reps)) \
        + proj(COL_KBR, COL_QKVC) * jnp.tile(sbk_ref[...], (1, reps))
    kb_out[...] = kb.astype(BF16)
    vb_out[...] = proj(COL_VB, COL_QBR).astype(BF16)

    qkvc_out[...] = proj(COL_QKVC, COL_G).astype(BF16)

    for br in range(N_BRANCH):
        lo = COL_G + br * D_MODEL
        z = proj(lo, lo + D_MODEL)
        gates_out[:, br * D_MODEL:(br + 1) * D_MODEL] = (1.0 / (1.0 + jnp.exp(-z))).astype(BF16)


def _in_proj(x2, g_mix, wp, g_q, g_kv, wqT, wqrT, wk, wvT, tabs, B, S, tm):
    T = B * S
    nS = S // tm
    cqT, sqT, tk, cbq, sbq, cbk, sbk = tabs
    row_tab = pl.BlockSpec((tm, LANE), lambda i: (i % nS, 0))
    col_tab = pl.BlockSpec((MLA_HEAD_PAD, tm), lambda i: (0, i % nS))
    in_specs = [
        pl.BlockSpec((tm, D_MODEL), lambda i: (i, 0)),
        _resident(None), _resident(None), _resident(None), _resident(None),
        _resident(None), _resident(None), _resident(None), _resident(None),
        col_tab, col_tab, row_tab, row_tab, row_tab, row_tab, row_tab,
    ]
    HP = MLA_HEADS * MLA_HEAD_PAD
    HV = MLA_HEADS * MLA_V_DIM
    out_shape = [
        jax.ShapeDtypeStruct((B, HP, S), BF16),
        jax.ShapeDtypeStruct((T, HP), BF16),
        jax.ShapeDtypeStruct((B, HV, S), BF16),
        jax.ShapeDtypeStruct((T, DIL_WIDTH), BF16),
        jax.ShapeDtypeStruct((T, DIL_WIDTH), BF16),
        jax.ShapeDtypeStruct((T, DIL_WIDTH), BF16),
        jax.ShapeDtypeStruct((T, 3 * NA_WIDTH), BF16),
        jax.ShapeDtypeStruct((T, N_BRANCH * D_MODEL), BF16),
    ]
    out_specs = [
        pl.BlockSpec((1, HP, tm), lambda i: (i // nS, 0, i % nS)),
        pl.BlockSpec((tm, HP), lambda i: (i, 0)),
        pl.BlockSpec((1, HV, tm), lambda i: (i // nS, 0, i % nS)),
        pl.BlockSpec((tm, DIL_WIDTH), lambda i: (i, 0)),
        pl.BlockSpec((tm, DIL_WIDTH), lambda i: (i, 0)),
        pl.BlockSpec((tm, DIL_WIDTH), lambda i: (i, 0)),
        pl.BlockSpec((tm, 3 * NA_WIDTH), lambda i: (i, 0)),
        pl.BlockSpec((tm, N_BRANCH * D_MODEL), lambda i: (i, 0)),
    ]
    return pl.pallas_call(
        _in_proj_kernel,
        grid=(T // tm,),
        in_specs=in_specs,
        out_specs=out_specs,
        out_shape=out_shape,
        compiler_params=pltpu.CompilerParams(
            dimension_semantics=("arbitrary",), vmem_limit_bytes=VMEM_LIMIT),
        name="in_proj",
    )(x2, g_mix, wp, g_q, g_kv, wqT, wqrT, wk, wvT, cqT, sqT, tk, cbq, sbq, cbk, sbk)


def _mla_kernel(qT_ref, k_ref, vT_ref, oT_ref, *, tk, n_chunks):
    q = qT_ref[0]
    tq = q.shape[1]

    def body(c, carry):
        m, l, acc = carry
        off = pl.multiple_of(c * tk, tk)
        k = k_ref[0, pl.ds(off, tk), :]
        s = _dot(k, q)
        m_new = jnp.maximum(m, jnp.max(s, axis=0, keepdims=True))
        alpha = jnp.exp2(m - m_new)
        p = jnp.exp2(s - m_new)
        l = alpha * l + jnp.sum(p, axis=0, keepdims=True)
        v = vT_ref[0, :, pl.ds(off, tk)]
        acc = alpha * acc + _dot(v, p.astype(BF16))
        return m_new, l, acc

    m0 = jnp.full((1, tq), -jnp.inf, F32)
    l0 = jnp.zeros((1, tq), F32)
    a0 = jnp.zeros((MLA_V_DIM, tq), F32)
    _, l, acc = lax.fori_loop(0, n_chunks, body, (m0, l0, a0))
    oT_ref[0] = (acc / l).astype(BF16)


def _mla(qT, k, vT, B, S, tq, tk):
    grid = (B, MLA_HEADS, S // tq)
    return pl.pallas_call(
        functools.partial(_mla_kernel, tk=tk, n_chunks=S // tk),
        grid=grid,
        in_specs=[
            pl.BlockSpec((1, MLA_HEAD_PAD, tq), lambda b, h, i: (b, h, i)),
            pl.BlockSpec((1, S, MLA_HEAD_PAD), lambda b, h, i: (b, 0, h)),
            pl.BlockSpec((1, MLA_V_DIM, S), lambda b, h, i: (b, h, 0)),
        ],
        out_specs=pl.BlockSpec((1, MLA_V_DIM, tq), lambda b, h, i: (b, h, i)),
        out_shape=jax.ShapeDtypeStruct((B, MLA_HEADS * MLA_V_DIM, S), BF16),
        compiler_params=pltpu.CompilerParams(
            dimension_semantics=("arbitrary", "arbitrary", "arbitrary"),
            vmem_limit_bytes=VMEM_LIMIT),
        name="mla",
    )(qT, k, vT)


def _head_lane_mask(width):
    return lax.broadcasted_iota(jnp.int32, (1, width), 1) < HEAD_DIM


def _dilated_kernel(q_ref, k_ref, v_ref, o_ref, lse_ref, *, tq, win, radius, L):
    i = pl.program_id(2)
    start = jnp.clip(i * tq - radius, 0, L - win)
    start = pl.multiple_of(start, radius)
    qpos = i * tq + lax.broadcasted_iota(jnp.int32, (tq, win), 0)
    kpos = start + lax.broadcasted_iota(jnp.int32, (tq, win), 1)
    band = jnp.abs(qpos - kpos) <= radius
    first = _head_lane_mask(LANE)
    for pair in range(DIL_HEADS_PER_GROUP // 2):
        lanes = slice(pair * LANE, (pair + 1) * LANE)
        qp = q_ref[0, :, lanes]
        kp = k_ref[0, pl.ds(start, win), lanes]
        vp = v_ref[0, pl.ds(start, win), lanes]
        o_pair = None
        lse_pair = None
        for hh in range(2):
            sel = first if hh == 0 else jnp.logical_not(first)
            qm = jnp.where(sel, qp, jnp.zeros_like(qp))
            vm = jnp.where(sel, vp, jnp.zeros_like(vp))
            s = jnp.where(band, _dot_nt(qm, kp), MASK_VALUE)
            m = jnp.max(s, axis=1, keepdims=True)
            p = jnp.exp2(s - m)
            l = jnp.sum(p, axis=1, keepdims=True)
            o = _dot(p.astype(BF16), vm) / l
            lse = m + jnp.log2(l)
            if hh == 0:
                o_pair = o
                lse_pair = lse
            else:
                o_pair = o_pair + o
                lse_pair = jnp.where(first, lse_pair, lse)
        o_ref[0, :, lanes] = o_pair.astype(BF16)
        lse_ref[0, :, lanes] = jnp.broadcast_to(lse_pair, (tq, LANE))


def _dilated_group(qb, kb, vb, B, S, group, dilation, radius):
    L = S // dilation
    tq = min(256, L)
    win = tq + 2 * radius
    assert L >= win and L % tq == 0
    ngrp = len(DIL_PATTERNS)
    qv = qb.reshape(B, L, dilation * DIL_WIDTH)
    kv = kb.reshape(B, L, dilation * DIL_WIDTH)
    vv = vb.reshape(B, L, dilation * DIL_WIDTH)
    GW = DIL_GROUP_WIDTH
    o, lse = pl.pallas_call(
        functools.partial(_dilated_kernel, tq=tq, win=win, radius=radius, L=L),
        grid=(B, dilation, L // tq),
        in_specs=[
            pl.BlockSpec((1, tq, GW), lambda b, r, i: (b, i, r * ngrp + group)),
            pl.BlockSpec((1, L, GW), lambda b, r, i: (b, 0, r * ngrp + group)),
            pl.BlockSpec((1, L, GW), lambda b, r, i: (b, 0, r * ngrp + group)),
        ],
        out_specs=[
            pl.BlockSpec((1, tq, GW), lambda b, r, i: (b, i, r)),
            pl.BlockSpec((1, tq, GW), lambda b, r, i: (b, i, r)),
        ],
        out_shape=[
            jax.ShapeDtypeStruct((B, L, dilation * GW), BF16),
            jax.ShapeDtypeStruct((B, L, dilation * GW), F32),
        ],
        compiler_params=pltpu.CompilerParams(
            dimension_semantics=("arbitrary", "arbitrary", "arbitrary"),
            vmem_limit_bytes=VMEM_LIMIT),
        name=f"dilated_g{group}",
    )(qv, kv, vv)
    return o.reshape(B * S, GW), lse.reshape(B * S, GW)


def _na_kernel(q_ref, k_ref, v_ref, bias_ref, o_ref, *, rows):
    i = pl.program_id(2)
    start_row = jnp.clip(i * NA_ROWS_PER_BLOCK - NA_KH // 2, 0, rows - NA_WINDOW_ROWS)
    start = pl.multiple_of(start_row * GRID_W, GRID_W)
    win = NA_WINDOW_ROWS * GRID_W
    qp = q_ref[0]
    kp = k_ref[0, pl.ds(start, win), :]
    vp = v_ref[0, pl.ds(start, win), :]
    first = _head_lane_mask(LANE)
    o_pair = None
    for hh in range(2):
        sel = first if hh == 0 else jnp.logical_not(first)
        qm = jnp.where(sel, qp, jnp.zeros_like(qp))
        vm = jnp.where(sel, vp, jnp.zeros_like(vp))
        s = _dot_nt(qm, kp) + bias_ref[0, hh]
        m = jnp.max(s, axis=1, keepdims=True)
        p = jnp.exp2(s - m)
        l = jnp.sum(p, axis=1, keepdims=True)
        o = _dot(p.astype(BF16), vm) / l
        o_pair = o if hh == 0 else o_pair + o
    o_ref[0] = o_pair.astype(BF16)


def _na_bias(rpb, rows):
    nb = rows // NA_ROWS_PER_BLOCK
    out = []
    for blk in (0, 1, nb - 1):
        R = blk * NA_ROWS_PER_BLOCK
        start = min(max(R - NA_KH // 2, 0), rows - NA_WINDOW_ROWS)
        qr = R + np.arange(NA_ROWS_PER_BLOCK)[:, None, None, None]
        qc = np.arange(GRID_W)[None, :, None, None]
        kr = start + np.arange(NA_WINDOW_ROWS)[None, None, :, None]
        kc = np.arange(GRID_W)[None, None, None, :]
        r0 = np.clip(qr - NA_KH // 2, 0, rows - NA_KH)
        cs = np.clip(qc - NA_KW // 2, 0, GRID_W - NA_KW)
        valid = (kr >= r0) & (kr < r0 + NA_KH) & (kc >= cs) & (kc < cs + NA_KW)
        dr = np.clip(kr - qr + (NA_KH - 1), 0, 2 * NA_KH - 2)
        dc = np.clip(kc - qc + (NA_KW - 1), 0, 2 * NA_KW - 2)
        shape = (NA_ROWS_PER_BLOCK, GRID_W, NA_WINDOW_ROWS, GRID_W)
        valid = np.broadcast_to(valid, shape).reshape(NA_ROWS_PER_BLOCK * GRID_W, -1)
        dr = np.broadcast_to(dr, shape).reshape(valid.shape)
        dc = np.broadcast_to(dc, shape).reshape(valid.shape)
        b = rpb[:, dr, dc].astype(F32) * LOG2E
        out.append(jnp.where(valid[None], b, MASK_VALUE))
    return jnp.stack(out)


def _na(qkvc, bias, B, S):
    rows = S // GRID_W
    nb = rows // NA_ROWS_PER_BLOCK
    assert nb >= 3
    tq = NA_ROWS_PER_BLOCK * GRID_W
    win = NA_WINDOW_ROWS * GRID_W
    npair = NA_HEADS // 2
    qv = qkvc.reshape(B, S, 3 * NA_WIDTH)

    def variant(i):
        return jnp.where(i == 0, 0, jnp.where(i == nb - 1, 2, 1))

    return pl.pallas_call(
        functools.partial(_na_kernel, rows=rows),
        grid=(B, npair, nb),
        in_specs=[
            pl.BlockSpec((1, tq, LANE), lambda b, p, i: (b, i, p)),
            pl.BlockSpec((1, S, LANE), lambda b, p, i: (b, 0, npair + p)),
            pl.BlockSpec((1, S, LANE), lambda b, p, i: (b, 0, 2 * npair + p)),
            pl.BlockSpec((1, 2, tq, win), lambda b, p, i: (variant(i), p, 0, 0)),
        ],
        out_specs=pl.BlockSpec((1, tq, LANE), lambda b, p, i: (b, i, p)),
        out_shape=jax.ShapeDtypeStruct((B, S, NA_WIDTH), BF16),
        compiler_params=pltpu.CompilerParams(
            dimension_semantics=("arbitrary", "arbitrary", "arbitrary"),
            vmem_limit_bytes=VMEM_LIMIT),
        name="na",
    )(qv, qv, qv, bias).reshape(B * S, NA_WIDTH)


def _merge_kernel(x_ref, yaT_ref, o1_ref, o2_ref, o3_ref, l1_ref, l2_ref, l3_ref, yc_ref, g_ref,
                  wpa_ref, wpb_ref, wpc_ref, wo_ref, out_ref):
    a = _dot_tn(yaT_ref[0], wpa_ref[...])
    l1, l2, l3 = l1_ref[...], l2_ref[...], l3_ref[...]
    mx = jnp.maximum(jnp.maximum(l1, l2), l3)
    w1, w2, w3 = jnp.exp2(l1 - mx), jnp.exp2(l2 - mx), jnp.exp2(l3 - mx)
    yb = (w1 * o1_ref[...].astype(F32) + w2 * o2_ref[...].astype(F32)
          + w3 * o3_ref[...].astype(F32)) / (w1 + w2 + w3)
    bm = _dot(yb.astype(BF16), wpb_ref[...])
    cm = _dot(yc_ref[...], wpc_ref[...])
    D = D_MODEL
    merged = (g_ref[:, 0:D].astype(F32) * a + g_ref[:, D:2 * D].astype(F32) * bm
              + g_ref[:, 2 * D:3 * D].astype(F32) * cm)
    out_ref[...] = x_ref[...] + _dot(merged.astype(BF16), wo_ref[...])


def _merge(x2, yaT, obs, lses, yc, gates, wpa, wpb, wpc, wo, B, S, tm):
    T = B * S
    nS = S // tm
    HV = MLA_HEADS * MLA_V_DIM
    GW = DIL_GROUP_WIDTH
    row = lambda w: pl.BlockSpec((tm, w), lambda i: (i, 0))
    return pl.pallas_call(
        _merge_kernel,
        grid=(T // tm,),
        in_specs=[
            row(D_MODEL),
            pl.BlockSpec((1, HV, tm), lambda i: (i // nS, 0, i % nS)),
            row(GW), row(GW), row(GW), row(GW), row(GW), row(GW),
            row(NA_WIDTH), row(N_BRANCH * D_MODEL),
            _resident(None), _resident(None), _resident(None), _resident(None),
        ],
        out_specs=row(D_MODEL),
        out_shape=jax.ShapeDtypeStruct((T, D_MODEL), F32),
        compiler_params=pltpu.CompilerParams(
            dimension_semantics=("arbitrary",), vmem_limit_bytes=VMEM_LIMIT),
        name="merge",
    )(x2, yaT, *obs, *lses, yc, gates, wpa, wpb, wpc, wo)


def _ffn_kernel(x_ref, g_ref, w1_ref, w3_ref, w2_ref, gf_ref, out_ref, *, final):
    x = x_ref[...]
    h = _rms(x, g_ref[...]).astype(BF16)
    u = _dot(h, w1_ref[...])
    v = _dot(h, w3_ref[...])
    a = (u / (1.0 + jnp.exp(-u)) * v).astype(BF16)
    y = x + _dot(a, w2_ref[...])
    if final:
        y = _rms(y, gf_ref[...])
    out_ref[...] = y


def _ffn(x2, g_ffn, w1, w3, w2, g_final, tm, final):
    T = x2.shape[0]
    row = pl.BlockSpec((tm, D_MODEL), lambda i: (i, 0))
    return pl.pallas_call(
        functools.partial(_ffn_kernel, final=final),
        grid=(T // tm,),
        in_specs=[row, _resident(None), _resident(None), _resident(None), _resident(None),
                  _resident(None)],
        out_specs=row,
        out_shape=jax.ShapeDtypeStruct((T, D_MODEL), F32),
        compiler_params=pltpu.CompilerParams(
            dimension_semantics=("arbitrary",), vmem_limit_bytes=VMEM_LIMIT),
        name="ffn",
    )(x2, g_ffn, w1, w3, w2, g_final)


def _rot_half_cols(w, dim):
    k = w.shape[0]
    w3 = w.reshape(k, -1, dim)
    return jnp.concatenate([-w3[..., dim // 2:], w3[..., :dim // 2]], axis=-1).reshape(k, -1)


def _pack_layer(w_in, w_uq, w_ukv):
    c_q, c_kv, k_r, qkv_b, qkv_c, gate = jnp.split(
        w_in,
        (MLA_Q_RANK, MLA_Q_RANK + MLA_KV_RANK, MLA_Q_RANK + MLA_KV_RANK + MLA_ROPE_DIM,
         MLA_Q_RANK + MLA_KV_RANK + MLA_ROPE_DIM + 3 * DIL_WIDTH,
         MLA_Q_RANK + MLA_KV_RANK + MLA_ROPE_DIM + 3 * DIL_WIDTH + 3 * NA_WIDTH), axis=1)
    q_b, k_b, v_b = jnp.split(qkv_b, 3, axis=1)
    q_c, k_c, v_c = jnp.split(qkv_c, 3, axis=1)
    q_c = q_c * (HEAD_DIM ** -0.5 * LOG2E)
    pad = jnp.zeros((D_MODEL, COL_QB - (MLA_Q_RANK + MLA_KV_RANK + 2 * MLA_ROPE_DIM)), F32)
    wp = jnp.concatenate(
        [c_q, c_kv, k_r, _rot_half_cols(k_r, MLA_ROPE_DIM), pad,
         q_b, k_b, v_b, _rot_half_cols(q_b, HEAD_DIM), _rot_half_cols(k_b, HEAD_DIM),
         q_c, k_c, v_c, gate], axis=1).astype(BF16)
    assert wp.shape[1] == COL_END

    uq = w_uq.reshape(MLA_Q_RANK, MLA_HEADS, MLA_NOPE_DIM + MLA_ROPE_DIM)
    zpad = jnp.zeros((MLA_Q_RANK, MLA_HEADS, MLA_HEAD_PAD - MLA_NOPE_DIM - MLA_ROPE_DIM), F32)
    q_main = jnp.concatenate([uq, zpad], axis=-1)
    rope = uq[..., MLA_NOPE_DIM:]
    rope_rot = jnp.concatenate([-rope[..., MLA_ROPE_DIM // 2:], rope[..., :MLA_ROPE_DIM // 2]], -1)
    q_rot = jnp.concatenate([jnp.zeros_like(uq[..., :MLA_NOPE_DIM]), rope_rot, zpad], axis=-1)
    HP = MLA_HEADS * MLA_HEAD_PAD
    wqT = q_main.reshape(MLA_Q_RANK, HP).T.astype(BF16)
    wqrT = q_rot.reshape(MLA_Q_RANK, HP).T.astype(BF16)

    ukv = w_ukv.reshape(MLA_KV_RANK, MLA_HEADS, MLA_NOPE_DIM + MLA_V_DIM)
    k_nope = jnp.concatenate(
        [ukv[..., :MLA_NOPE_DIM],
         jnp.zeros((MLA_KV_RANK, MLA_HEADS, MLA_HEAD_PAD - MLA_NOPE_DIM), F32)], axis=-1)
    place = np.zeros((LANE, MLA_HEADS, MLA_HEAD_PAD), np.float32)
    for j in range(2 * MLA_ROPE_DIM):
        place[j, :, MLA_NOPE_DIM + j % MLA_ROPE_DIM] = 1.0
    wk = jnp.concatenate([k_nope.reshape(MLA_KV_RANK, HP), jnp.asarray(place).reshape(LANE, HP)],
                         axis=0).astype(BF16)
    wvT = ukv[..., MLA_NOPE_DIM:].reshape(MLA_KV_RANK, MLA_HEADS * MLA_V_DIM).T.astype(BF16)
    return wp, wqT, wqrT, wk, wvT


def _rope_cos_sin(seq_len, dim):
    pos = jnp.arange(seq_len, dtype=F32)
    inv = jnp.power(ROPE_THETA, -jnp.arange(0, dim, 2, dtype=F32) / dim)
    ang = pos[:, None] * inv[None, :]
    return jnp.cos(ang), jnp.sin(ang)


def _tables(S):
    cos_a, sin_a = _rope_cos_sin(S, MLA_ROPE_DIM)
    qs = (MLA_NOPE_DIM + MLA_ROPE_DIM) ** -0.5 * LOG2E
    ones = jnp.ones((S, MLA_NOPE_DIM), F32)
    zeros_n = jnp.zeros((S, MLA_NOPE_DIM), F32)
    zpad = jnp.zeros((S, MLA_HEAD_PAD - MLA_NOPE_DIM - MLA_ROPE_DIM), F32)
    cqT = (jnp.concatenate([ones, cos_a, cos_a, zpad], axis=1) * qs).T
    sqT = (jnp.concatenate([zeros_n, sin_a, sin_a, zpad], axis=1) * qs).T
    tk = jnp.concatenate([cos_a, cos_a, sin_a, sin_a, jnp.zeros((S, LANE - 2 * MLA_ROPE_DIM), F32)],
                         axis=1)
    cos_b, sin_b = _rope_cos_sin(S, HEAD_DIM)
    cb = jnp.tile(cos_b, (1, LANE // (HEAD_DIM // 2)))
    sb = jnp.tile(sin_b, (1, LANE // (HEAD_DIM // 2)))
    bs = HEAD_DIM ** -0.5 * LOG2E
    return cqT, sqT, tk, cb * bs, sb * bs, cb, sb


def kernel(x, w_in, g_mix, g_q, g_kv, w_uq, w_ukv, rpb, w_pa, w_pb, w_pc, w_o, g_ffn, w1, w3, w2,
           g_final):
    B, S, D = x.shape
    depth = w_in.shape[0]
    T = B * S
    tm_proj = 256
    tm_post = 512
    tabs = _tables(S)
    x2 = x.reshape(T, D)
    for l in range(depth):
        wp, wqT, wqrT, wk, wvT = _pack_layer(w_in[l], w_uq[l], w_ukv[l])
        qT, k, vT, qb, kb, vb, qkvc, gates = _in_proj(
            x2, g_mix[l][None], wp, g_q[l][None], g_kv[l][None], wqT, wqrT, wk, wvT, tabs,
            B, S, tm_proj)
        yaT = _mla(qT, k.reshape(B, S, -1), vT, B, S, tq=512, tk=512)
        obs, lses = [], []
        for grp, (window, dilation) in enumerate(DIL_PATTERNS):
            o, lse = _dilated_group(qb, kb, vb, B, S, grp, dilation, window // (2 * dilation))
            obs.append(o)
            lses.append(lse)
        yc = _na(qkvc, _na_bias(rpb[l], S // GRID_W), B, S)
        x2 = _merge(x2, yaT, obs, lses, yc, gates, w_pa[l].astype(BF16), w_pb[l].astype(BF16),
                    w_pc[l].astype(BF16), w_o[l].astype(BF16), B, S, tm_post)
        x2 = _ffn(x2, g_ffn[l][None], w1[l].astype(BF16), w3[l].astype(BF16), w2[l].astype(BF16),
                  g_final[None], tm_post, final=(l == depth - 1))
    return x2.reshape(B, S, D)
```

```python
import functools
import math

import jax
import jax.numpy as jnp
import numpy as np
from jax import lax
from jax.experimental import pallas as pl
from jax.experimental.pallas import tpu as pltpu

F32 = jnp.float32
BF16 = jnp.bfloat16

D_MODEL = 1024
HEAD_DIM = 64
ROPE_THETA = 10000.0
EPS = 1e-6
MASK_VALUE = -1e30
LOG2E = 1.4426950408889634

MLA_HEADS = 8
MLA_Q_RANK = 256
MLA_KV_RANK = 128
MLA_NOPE_DIM = 64
MLA_ROPE_DIM = 32
MLA_V_DIM = 64
MLA_HEAD_PAD = 128

DIL_PATTERNS = ((128, 1), (512, 4), (2048, 16))
DIL_HEADS_PER_GROUP = 4
DIL_HEADS = DIL_HEADS_PER_GROUP * len(DIL_PATTERNS)
DIL_WIDTH = DIL_HEADS * HEAD_DIM
DIL_GROUP_WIDTH = DIL_HEADS_PER_GROUP * HEAD_DIM

NA_HEADS = 8
NA_KH = 8
NA_KW = 16
GRID_W = 64
NA_WIDTH = NA_HEADS * HEAD_DIM
NA_ROWS_PER_BLOCK = 8
NA_WINDOW_ROWS = 16

N_BRANCH = 3
D_FF = -(-(8 * D_MODEL) // (3 * 256)) * 256

LANE = 128

COL_C = 0
COL_QB = 512
COL_KB = COL_QB + DIL_WIDTH
COL_VB = COL_KB + DIL_WIDTH
COL_QBR = COL_VB + DIL_WIDTH
COL_KBR = COL_QBR + DIL_WIDTH
COL_QKVC = COL_KBR + DIL_WIDTH
COL_G = COL_QKVC + 3 * NA_WIDTH
COL_END = COL_G + N_BRANCH * D_MODEL

VMEM_LIMIT = 56 * 1024 * 1024


def _dot(a, b):
    return jnp.dot(a, b, preferred_element_type=F32)


def _dot_nt(a, b):
    return lax.dot_general(a, b, (((1,), (1,)), ((), ())), preferred_element_type=F32)


def _dot_tn(a, b):
    return lax.dot_general(a, b, (((0,), (0,)), ((), ())), preferred_element_type=F32)


def _rms(x, g):
    return x * lax.rsqrt(jnp.mean(x * x, axis=-1, keepdims=True) + EPS) * g


def _resident(shape):
    del shape
    return pl.BlockSpec(memory_space=pltpu.VMEM)


def _in_proj_kernel(x_ref, g_ref, w_ref, gq_ref, gkv_ref, wqT_ref, wqrT_ref, wk_ref, wvT_ref,
                    cqT_ref, sqT_ref, tk_ref, cbq_ref, sbq_ref, cbk_ref, sbk_ref,
                    qT_out, k_out, vT_out, qb_out, kb_out, vb_out, qkvc_out, gates_out):
    h = _rms(x_ref[...], g_ref[...]).astype(BF16)

    def proj(lo, hi):
        return _dot(h, w_ref[:, lo:hi])

    c = proj(COL_C, COL_QB)
    cqn = _rms(c[:, :MLA_Q_RANK], gq_ref[...]).astype(BF16)
    ckvn = _rms(c[:, MLA_Q_RANK:MLA_Q_RANK + MLA_KV_RANK], gkv_ref[...]).astype(BF16)
    kr = (c[:, MLA_Q_RANK + MLA_KV_RANK:] * tk_ref[...]).astype(BF16)
    k_out[...] = _dot(jnp.concatenate([ckvn, kr], axis=1), wk_ref[...]).astype(BF16)
    vT_out[0] = _dot_nt(wvT_ref[...], ckvn).astype(BF16)
    q_main = _dot_nt(wqT_ref[...], cqn)
    q_rot = _dot_nt(wqrT_ref[...], cqn)
    cq = cqT_ref[...]
    sq = sqT_ref[...]
    for hd in range(MLA_HEADS):
        rows = slice(hd * MLA_HEAD_PAD, (hd + 1) * MLA_HEAD_PAD)
        qT_out[0, rows, :] = (q_main[rows] * cq + q_rot[rows] * sq).astype(BF16)

    reps = DIL_WIDTH // LANE
    qb = proj(COL_QB, COL_KB) * jnp.tile(cbq_ref[...], (1, reps)) \
        + proj(COL_QBR, COL_KBR) * jnp.tile(sbq_ref[...], (1, reps))
    qb_out[...] = qb.astype(BF16)
    kb = proj(COL_KB, COL_VB) * jnp.tile(cbk_ref[...], (1, reps)) \
        + proj(COL_KBR, COL_QKVC) * jnp.tile(sbk_ref[...], (1, reps))
    kb_out[...] = kb.astype(BF16)
    vb_out[...] = proj(COL_VB, COL_QBR).astype(BF16)

    qkvc_out[...] = proj(COL_QKVC, COL_G).astype(BF16)

    for br in range(N_BRANCH):
        lo = COL_G + br * D_MODEL
        z = proj(lo, lo + D_MODEL)
        gates_out[:, br * D_MODEL:(br + 1) * D_MODEL] = (1.0 / (1.0 + jnp.exp(-z))).astype(BF16)


def _in_proj(x2, g_mix, wp, g_q, g_kv, wqT, wqrT, wk, wvT, tabs, B, S, tm):
    T = B * S
    nS = S // tm
    cqT, sqT, tk, cbq, sbq, cbk, sbk = tabs
    row_tab = pl.BlockSpec((tm, LANE), lambda i: (i % nS, 0))
    col_tab = pl.BlockSpec((MLA_HEAD_PAD, tm), lambda i: (0, i % nS))
    in_specs = [
        pl.BlockSpec((tm, D_MODEL), lambda i: (i, 0)),
        _resident(None), _resident(None), _resident(None), _resident(None),
        _resident(None), _resident(None), _resident(None), _resident(None),
        col_tab, col_tab, row_tab, row_tab, row_tab, row_tab, row_tab,
    ]
    HP = MLA_HEADS * MLA_HEAD_PAD
    HV = MLA_HEADS * MLA_V_DIM
    out_shape = [
        jax.ShapeDtypeStruct((B, HP, S), BF16),
        jax.ShapeDtypeStruct((T, HP), BF16),
        jax.ShapeDtypeStruct((B, HV, S), BF16),
        jax.ShapeDtypeStruct((T, DIL_WIDTH), BF16),
        jax.ShapeDtypeStruct((T, DIL_WIDTH), BF16),
        jax.ShapeDtypeStruct((T, DIL_WIDTH), BF16),
        jax.ShapeDtypeStruct((T, 3 * NA_WIDTH), BF16),
        jax.ShapeDtypeStruct((T, N_BRANCH * D_MODEL), BF16),
    ]
    out_specs = [
        pl.BlockSpec((1, HP, tm), lambda i: (i // nS, 0, i % nS)),
        pl.BlockSpec((tm, HP), lambda i: (i, 0)),
        pl.BlockSpec((1, HV, tm), lambda i: (i // nS, 0, i % nS)),
        pl.BlockSpec((tm, DIL_WIDTH), lambda i: (i, 0)),
        pl.BlockSpec((tm, DIL_WIDTH), lambda i: (i, 0)),
        pl.BlockSpec((tm, DIL_WIDTH), lambda i: (i, 0)),
        pl.BlockSpec((tm, 3 * NA_WIDTH), lambda i: (i, 0)),
        pl.BlockSpec((tm, N_BRANCH * D_MODEL), lambda i: (i, 0)),
    ]
    return pl.pallas_call(
        _in_proj_kernel,
        grid=(T // tm,),
        in_specs=in_specs,
        out_specs=out_specs,
        out_shape=out_shape,
        compiler_params=pltpu.CompilerParams(
            dimension_semantics=("arbitrary",), vmem_limit_bytes=VMEM_LIMIT),
        name="in_proj",
    )(x2, g_mix, wp, g_q, g_kv, wqT, wqrT, wk, wvT, cqT, sqT, tk, cbq, sbq, cbk, sbk)


def _mla_kernel(qT_ref, k_ref, vT_ref, oT_ref, s0_ref, s1_ref, p0_ref, p1_ref, *, tk, n_chunks):
    q = qT_ref[0]
    tq = q.shape[1]
    s_slots = (s0_ref, s1_ref)
    p_slots = (p0_ref, p1_ref)

    def scores(c, slot):
        off = pl.multiple_of(c * tk, tk)
        s = _dot(k_ref[0, pl.ds(off, tk), :], q)
        s_slots[slot][...] = s
        return jnp.max(s, axis=0, keepdims=True)

    def softmax(slot, m, l, cmax):
        m_new = jnp.maximum(m, cmax)
        alpha = jnp.exp2(m - m_new)
        p = jnp.exp2(s_slots[slot][...] - m_new)
        p_slots[slot][...] = p.astype(BF16)
        return m_new, alpha * l + jnp.sum(p, axis=0, keepdims=True), alpha

    def values(c, slot, alpha, acc):
        off = pl.multiple_of(c * tk, tk)
        return alpha * acc + _dot(vT_ref[0, :, pl.ds(off, tk)], p_slots[slot][...])

    def step(c, parity, carry, with_scores=True):
        m, l, acc, cmax, alpha_prev = carry
        cmax_next = scores(c + 1, 1 - parity) if with_scores else cmax
        m, l, alpha = softmax(parity, m, l, cmax)
        acc = values(c - 1, 1 - parity, alpha_prev, acc)
        return m, l, acc, cmax_next, alpha

    assert n_chunks % 2 == 0 and n_chunks >= 4
    cmax0 = scores(0, 0)
    cmax1 = scores(1, 1)
    m0 = jnp.full((1, tq), -jnp.inf, F32)
    m, l, alpha = softmax(0, m0, jnp.zeros((1, tq), F32), cmax0)
    carry = (m, l, jnp.zeros((MLA_V_DIM, tq), F32), cmax1, alpha)

    def body(j, carry):
        c = 2 * j + 1
        carry = step(c, 1, carry)
        return step(c + 1, 0, carry)

    carry = lax.fori_loop(0, n_chunks // 2 - 1, body, carry)
    last = n_chunks - 1
    m, l, acc, _, alpha = step(last, 1, carry, with_scores=False)
    acc = values(last, 1, alpha, acc)
    oT_ref[0] = (acc / l).astype(BF16)


def _mla(qT, k, vT, B, S, tq, tk):
    grid = (B, MLA_HEADS, S // tq)
    return pl.pallas_call(
        functools.partial(_mla_kernel, tk=tk, n_chunks=S // tk),
        grid=grid,
        scratch_shapes=[pltpu.VMEM((tk, tq), F32), pltpu.VMEM((tk, tq), F32),
                        pltpu.VMEM((tk, tq), BF16), pltpu.VMEM((tk, tq), BF16)],
        in_specs=[
            pl.BlockSpec((1, MLA_HEAD_PAD, tq), lambda b, h, i: (b, h, i)),
            pl.BlockSpec((1, S, MLA_HEAD_PAD), lambda b, h, i: (b, 0, h)),
            pl.BlockSpec((1, MLA_V_DIM, S), lambda b, h, i: (b, h, 0)),
        ],
        out_specs=pl.BlockSpec((1, MLA_V_DIM, tq), lambda b, h, i: (b, h, i)),
        out_shape=jax.ShapeDtypeStruct((B, MLA_HEADS * MLA_V_DIM, S), BF16),
        compiler_params=pltpu.CompilerParams(
            dimension_semantics=("arbitrary", "arbitrary", "arbitrary"),
            vmem_limit_bytes=VMEM_LIMIT),
        name="mla",
    )(qT, k, vT)


def _head_lane_mask(width):
    return lax.broadcasted_iota(jnp.int32, (1, width), 1) < HEAD_DIM


def _dilated_kernel(q_ref, k_ref, v_ref, o_ref, lse_ref, *, tq, win, radius, L):
    i = pl.program_id(2)
    start = jnp.clip(i * tq - radius, 0, L - win)
    start = pl.multiple_of(start, radius)
    qpos = i * tq + lax.broadcasted_iota(jnp.int32, (tq, win), 0)
    kpos = start + lax.broadcasted_iota(jnp.int32, (tq, win), 1)
    band = jnp.abs(qpos - kpos) <= radius
    first = _head_lane_mask(LANE)
    for pair in range(DIL_HEADS_PER_GROUP // 2):
        lanes = slice(pair * LANE, (pair + 1) * LANE)
        qp = q_ref[0, :, lanes]
        kp = k_ref[0, pl.ds(start, win), lanes]
        vp = v_ref[0, pl.ds(start, win), lanes]
        o_pair = None
        lse_pair = None
        for hh in range(2):
            sel = first if hh == 0 else jnp.logical_not(first)
            qm = jnp.where(sel, qp, jnp.zeros_like(qp))
            vm = jnp.where(sel, vp, jnp.zeros_like(vp))
            s = jnp.where(band, _dot_nt(qm, kp), MASK_VALUE)
            m = jnp.max(s, axis=1, keepdims=True)
            p = jnp.exp2(s - m)
            l = jnp.sum(p, axis=1, keepdims=True)
            o = _dot(p.astype(BF16), vm) / l
            lse = m + jnp.log2(l)
            if hh == 0:
                o_pair = o
                lse_pair = lse
            else:
                o_pair = o_pair + o
                lse_pair = jnp.where(first, lse_pair, lse)
        o_ref[0, :, lanes] = o_pair.astype(BF16)
        lse_ref[0, :, lanes] = jnp.broadcast_to(lse_pair, (tq, LANE))


def _dilated_group(qb, kb, vb, B, S, group, dilation, radius):
    L = S // dilation
    tq = min(256, L)
    win = tq + 2 * radius
    assert L >= win and L % tq == 0
    ngrp = len(DIL_PATTERNS)
    qv = qb.reshape(B, L, dilation * DIL_WIDTH)
    kv = kb.reshape(B, L, dilation * DIL_WIDTH)
    vv = vb.reshape(B, L, dilation * DIL_WIDTH)
    GW = DIL_GROUP_WIDTH
    o, lse = pl.pallas_call(
        functools.partial(_dilated_kernel, tq=tq, win=win, radius=radius, L=L),
        grid=(B, dilation, L // tq),
        in_specs=[
            pl.BlockSpec((1, tq, GW), lambda b, r, i: (b, i, r * ngrp + group)),
            pl.BlockSpec((1, L, GW), lambda b, r, i: (b, 0, r * ngrp + group)),
            pl.BlockSpec((1, L, GW), lambda b, r, i: (b, 0, r * ngrp + group)),
        ],
        out_specs=[
            pl.BlockSpec((1, tq, GW), lambda b, r, i: (b, i, r)),
            pl.BlockSpec((1, tq, GW), lambda b, r, i: (b, i, r)),
        ],
        out_shape=[
            jax.ShapeDtypeStruct((B, L, dilation * GW), BF16),
            jax.ShapeDtypeStruct((B, L, dilation * GW), F32),
        ],
        compiler_params=pltpu.CompilerParams(
            dimension_semantics=("arbitrary", "arbitrary", "arbitrary"),
            vmem_limit_bytes=VMEM_LIMIT),
        name=f"dilated_g{group}",
    )(qv, kv, vv)
    return o.reshape(B * S, GW), lse.reshape(B * S, GW)


def _na_kernel(q_ref, k_ref, v_ref, bias_ref, o_ref, *, rows):
    i = pl.program_id(2)
    start_row = jnp.clip(i * NA_ROWS_PER_BLOCK - NA_KH // 2, 0, rows - NA_WINDOW_ROWS)
    start = pl.multiple_of(start_row * GRID_W, GRID_W)
    win = NA_WINDOW_ROWS * GRID_W
    qp = q_ref[0]
    kp = k_ref[0, pl.ds(start, win), :]
    vp = v_ref[0, pl.ds(start, win), :]
    first = _head_lane_mask(LANE)
    o_pair = None
    for hh in range(2):
        sel = first if hh == 0 else jnp.logical_not(first)
        qm = jnp.where(sel, qp, jnp.zeros_like(qp))
        vm = jnp.where(sel, vp, jnp.zeros_like(vp))
        s = _dot_nt(qm, kp) + bias_ref[0, hh]
        m = jnp.max(s, axis=1, keepdims=True)
        p = jnp.exp2(s - m)
        l = jnp.sum(p, axis=1, keepdims=True)
        o = _dot(p.astype(BF16), vm) / l
        o_pair = o if hh == 0 else o_pair + o
    o_ref[0] = o_pair.astype(BF16)


def _na_bias(rpb, rows):
    nb = rows // NA_ROWS_PER_BLOCK
    n_dr, n_dc = 2 * NA_KH - 1, 2 * NA_KW - 1
    qc = np.arange(GRID_W)[:, None]
    kc = np.arange(GRID_W)[None, :]
    cs = np.clip(qc - NA_KW // 2, 0, GRID_W - NA_KW)
    col_ok = (kc >= cs) & (kc < cs + NA_KW)
    dc = kc - qc + (NA_KW - 1)
    rpb2 = rpb.astype(F32) * LOG2E
    tc = sum(rpb2[:, :, j, None, None] * jnp.asarray((dc == j) & col_ok, F32) for j in range(n_dc))
    sel = np.zeros((3, NA_ROWS_PER_BLOCK, NA_WINDOW_ROWS, n_dr), np.float32)
    for v, blk in enumerate((0, 1, nb - 1)):
        R = blk * NA_ROWS_PER_BLOCK
        start = min(max(R - NA_KH // 2, 0), rows - NA_WINDOW_ROWS)
        for a in range(NA_ROWS_PER_BLOCK):
            r0 = min(max(R + a - NA_KH // 2, 0), rows - NA_KH)
            for b in range(NA_WINDOW_ROWS):
                if r0 <= start + b < r0 + NA_KH:
                    sel[v, a, b, start + b - (R + a) + NA_KH - 1] = 1.0
    row_ok = sel.sum(-1) > 0
    bias = sum(jnp.asarray(sel[:, None, :, None, :, None, j]) * tc[None, :, j, None, :, None, :]
               for j in range(n_dr))
    ok = row_ok[:, None, :, None, :, None] & col_ok[None, None, None, :, None, :]
    bias = jnp.where(jnp.asarray(ok), bias, MASK_VALUE)
    return bias.reshape(3, NA_HEADS, NA_ROWS_PER_BLOCK * GRID_W, NA_WINDOW_ROWS * GRID_W)


def _na(qkvc, bias, B, S):
    rows = S // GRID_W
    nb = rows // NA_ROWS_PER_BLOCK
    assert nb >= 3
    tq = NA_ROWS_PER_BLOCK * GRID_W
    win = NA_WINDOW_ROWS * GRID_W
    npair = NA_HEADS // 2
    qv = qkvc.reshape(B, S, 3 * NA_WIDTH)

    def variant(i):
        return jnp.where(i == 0, 0, jnp.where(i == nb - 1, 2, 1))

    return pl.pallas_call(
        functools.partial(_na_kernel, rows=rows),
        grid=(B, npair, nb),
        in_specs=[
            pl.BlockSpec((1, tq, LANE), lambda b, p, i: (b, i, p)),
            pl.BlockSpec((1, S, LANE), lambda b, p, i: (b, 0, npair + p)),
            pl.BlockSpec((1, S, LANE), lambda b, p, i: (b, 0, 2 * npair + p)),
            pl.BlockSpec((1, 2, tq, win), lambda b, p, i: (variant(i), p, 0, 0)),
        ],
        out_specs=pl.BlockSpec((1, tq, LANE), lambda b, p, i: (b, i, p)),
        out_shape=jax.ShapeDtypeStruct((B, S, NA_WIDTH), BF16),
        compiler_params=pltpu.CompilerParams(
            dimension_semantics=("arbitrary", "arbitrary", "arbitrary"),
            vmem_limit_bytes=VMEM_LIMIT),
        name="na",
    )(qv, qv, qv, bias).reshape(B * S, NA_WIDTH)


def _merge_kernel(x_ref, yaT_ref, o1_ref, o2_ref, o3_ref, l1_ref, l2_ref, l3_ref, yc_ref, g_ref,
                  wpa_ref, wpb_ref, wpc_ref, wo_ref, out_ref):
    a = _dot_tn(yaT_ref[0], wpa_ref[...])
    l1, l2, l3 = l1_ref[...], l2_ref[...], l3_ref[...]
    mx = jnp.maximum(jnp.maximum(l1, l2), l3)
    w1, w2, w3 = jnp.exp2(l1 - mx), jnp.exp2(l2 - mx), jnp.exp2(l3 - mx)
    yb = (w1 * o1_ref[...].astype(F32) + w2 * o2_ref[...].astype(F32)
          + w3 * o3_ref[...].astype(F32)) / (w1 + w2 + w3)
    bm = _dot(yb.astype(BF16), wpb_ref[...])
    cm = _dot(yc_ref[...], wpc_ref[...])
    D = D_MODEL
    merged = (g_ref[:, 0:D].astype(F32) * a + g_ref[:, D:2 * D].astype(F32) * bm
              + g_ref[:, 2 * D:3 * D].astype(F32) * cm)
    out_ref[...] = x_ref[...] + _dot(merged.astype(BF16), wo_ref[...])


def _merge(x2, yaT, obs, lses, yc, gates, wpa, wpb, wpc, wo, B, S, tm):
    T = B * S
    nS = S // tm
    HV = MLA_HEADS * MLA_V_DIM
    GW = DIL_GROUP_WIDTH
    row = lambda w: pl.BlockSpec((tm, w), lambda i: (i, 0))
    return pl.pallas_call(
        _merge_kernel,
        grid=(T // tm,),
        in_specs=[
            row(D_MODEL),
            pl.BlockSpec((1, HV, tm), lambda i: (i // nS, 0, i % nS)),
            row(GW), row(GW), row(GW), row(GW), row(GW), row(GW),
            row(NA_WIDTH), row(N_BRANCH * D_MODEL),
            _resident(None), _resident(None), _resident(None), _resident(None),
        ],
        out_specs=row(D_MODEL),
        out_shape=jax.ShapeDtypeStruct((T, D_MODEL), F32),
        compiler_params=pltpu.CompilerParams(
            dimension_semantics=("arbitrary",), vmem_limit_bytes=VMEM_LIMIT),
        name="merge",
    )(x2, yaT, *obs, *lses, yc, gates, wpa, wpb, wpc, wo)


def _ffn_kernel(x_ref, g_ref, w1_ref, w3_ref, w2_ref, gf_ref, out_ref, *, final):
    x = x_ref[...]
    h = _rms(x, g_ref[...]).astype(BF16)
    u = _dot(h, w1_ref[...])
    v = _dot(h, w3_ref[...])
    a = (u / (1.0 + jnp.exp(-u)) * v).astype(BF16)
    y = x + _dot(a, w2_ref[...])
    if final:
        y = _rms(y, gf_ref[...])
    out_ref[...] = y


def _ffn(x2, g_ffn, w1, w3, w2, g_final, tm, final):
    T = x2.shape[0]
    row = pl.BlockSpec((tm, D_MODEL), lambda i: (i, 0))
    return pl.pallas_call(
        functools.partial(_ffn_kernel, final=final),
        grid=(T // tm,),
        in_specs=[row, _resident(None), _resident(None), _resident(None), _resident(None),
                  _resident(None)],
        out_specs=row,
        out_shape=jax.ShapeDtypeStruct((T, D_MODEL), F32),
        compiler_params=pltpu.CompilerParams(
            dimension_semantics=("arbitrary",), vmem_limit_bytes=VMEM_LIMIT),
        name="ffn",
    )(x2, g_ffn, w1, w3, w2, g_final)


def _rot_half_cols(w, dim):
    k = w.shape[0]
    w3 = w.reshape(k, -1, dim)
    return jnp.concatenate([-w3[..., dim // 2:], w3[..., :dim // 2]], axis=-1).reshape(k, -1)


def _pack_layer(w_in, w_uq, w_ukv):
    c_q, c_kv, k_r, qkv_b, qkv_c, gate = jnp.split(
        w_in,
        (MLA_Q_RANK, MLA_Q_RANK + MLA_KV_RANK, MLA_Q_RANK + MLA_KV_RANK + MLA_ROPE_DIM,
         MLA_Q_RANK + MLA_KV_RANK + MLA_ROPE_DIM + 3 * DIL_WIDTH,
         MLA_Q_RANK + MLA_KV_RANK + MLA_ROPE_DIM + 3 * DIL_WIDTH + 3 * NA_WIDTH), axis=1)
    q_b, k_b, v_b = jnp.split(qkv_b, 3, axis=1)
    q_c, k_c, v_c = jnp.split(qkv_c, 3, axis=1)
    q_c = q_c * (HEAD_DIM ** -0.5 * LOG2E)
    pad = jnp.zeros((D_MODEL, COL_QB - (MLA_Q_RANK + MLA_KV_RANK + 2 * MLA_ROPE_DIM)), F32)
    wp = jnp.concatenate(
        [c_q, c_kv, k_r, _rot_half_cols(k_r, MLA_ROPE_DIM), pad,
         q_b, k_b, v_b, _rot_half_cols(q_b, HEAD_DIM), _rot_half_cols(k_b, HEAD_DIM),
         q_c, k_c, v_c, gate], axis=1).astype(BF16)
    assert wp.shape[1] == COL_END

    uq = w_uq.reshape(MLA_Q_RANK, MLA_HEADS, MLA_NOPE_DIM + MLA_ROPE_DIM)
    zpad = jnp.zeros((MLA_Q_RANK, MLA_HEADS, MLA_HEAD_PAD - MLA_NOPE_DIM - MLA_ROPE_DIM), F32)
    q_main = jnp.concatenate([uq, zpad], axis=-1)
    rope = uq[..., MLA_NOPE_DIM:]
    rope_rot = jnp.concatenate([-rope[..., MLA_ROPE_DIM // 2:], rope[..., :MLA_ROPE_DIM // 2]], -1)
    q_rot = jnp.concatenate([jnp.zeros_like(uq[..., :MLA_NOPE_DIM]), rope_rot, zpad], axis=-1)
    HP = MLA_HEADS * MLA_HEAD_PAD
    wqT = q_main.reshape(MLA_Q_RANK, HP).T.astype(BF16)
    wqrT = q_rot.reshape(MLA_Q_RANK, HP).T.astype(BF16)

    ukv = w_ukv.reshape(MLA_KV_RANK, MLA_HEADS, MLA_NOPE_DIM + MLA_V_DIM)
    k_nope = jnp.concatenate(
        [ukv[..., :MLA_NOPE_DIM],
         jnp.zeros((MLA_KV_RANK, MLA_HEADS, MLA_HEAD_PAD - MLA_NOPE_DIM), F32)], axis=-1)
    place = np.zeros((LANE, MLA_HEADS, MLA_HEAD_PAD), np.float32)
    for j in range(2 * MLA_ROPE_DIM):
        place[j, :, MLA_NOPE_DIM + j % MLA_ROPE_DIM] = 1.0
    wk = jnp.concatenate([k_nope.reshape(MLA_KV_RANK, HP), jnp.asarray(place).reshape(LANE, HP)],
                         axis=0).astype(BF16)
    wvT = ukv[..., MLA_NOPE_DIM:].reshape(MLA_KV_RANK, MLA_HEADS * MLA_V_DIM).T.astype(BF16)
    return wp, wqT, wqrT, wk, wvT


def _rope_cos_sin(seq_len, dim):
    pos = jnp.arange(seq_len, dtype=F32)
    inv = jnp.power(ROPE_THETA, -jnp.arange(0, dim, 2, dtype=F32) / dim)
    ang = pos[:, None] * inv[None, :]
    return jnp.cos(ang), jnp.sin(ang)


def _tables(S):
    cos_a, sin_a = _rope_cos_sin(S, MLA_ROPE_DIM)
    qs = (MLA_NOPE_DIM + MLA_ROPE_DIM) ** -0.5 * LOG2E
    ones = jnp.ones((S, MLA_NOPE_DIM), F32)
    zeros_n = jnp.zeros((S, MLA_NOPE_DIM), F32)
    zpad = jnp.zeros((S, MLA_HEAD_PAD - MLA_NOPE_DIM - MLA_ROPE_DIM), F32)
    cqT = (jnp.concatenate([ones, cos_a, cos_a, zpad], axis=1) * qs).T
    sqT = (jnp.concatenate([zeros_n, sin_a, sin_a, zpad], axis=1) * qs).T
    tk = jnp.concatenate([cos_a, cos_a, sin_a, sin_a, jnp.zeros((S, LANE - 2 * MLA_ROPE_DIM), F32)],
                         axis=1)
    cos_b, sin_b = _rope_cos_sin(S, HEAD_DIM)
    cb = jnp.tile(cos_b, (1, LANE // (HEAD_DIM // 2)))
    sb = jnp.tile(sin_b, (1, LANE // (HEAD_DIM // 2)))
    bs = HEAD_DIM ** -0.5 * LOG2E
    return cqT, sqT, tk, cb * bs, sb * bs, cb, sb


def kernel(x, w_in, g_mix, g_q, g_kv, w_uq, w_ukv, rpb, w_pa, w_pb, w_pc, w_o, g_ffn, w1, w3, w2,
           g_final):
    B, S, D = x.shape
    depth = w_in.shape[0]
    T = B * S
    tm_proj = 256
    tm_post = 512
    tabs = _tables(S)
    x2 = x.reshape(T, D)
    for l in range(depth):
        wp, wqT, wqrT, wk, wvT = _pack_layer(w_in[l], w_uq[l], w_ukv[l])
        qT, k, vT, qb, kb, vb, qkvc, gates = _in_proj(
            x2, g_mix[l][None], wp, g_q[l][None], g_kv[l][None], wqT, wqrT, wk, wvT, tabs,
            B, S, tm_proj)
        yaT = _mla(qT, k.reshape(B, S, -1), vT, B, S, tq=512, tk=512)
        obs, lses = [], []
        for grp, (window, dilation) in enumerate(DIL_PATTERNS):
            o, lse = _dilated_group(qb, kb, vb, B, S, grp, dilation, window // (2 * dilation))
            obs.append(o)
            lses.append(lse)
        yc = _na(qkvc, _na_bias(rpb[l], S // GRID_W), B, S)
        x2 = _merge(x2, yaT, obs, lses, yc, gates, w_pa[l].astype(BF16), w_pb[l].astype(BF16),
                    w_pc[l].astype(BF16), w_o[l].astype(BF16), B, S, tm_post)
        x2 = _ffn(x2, g_ffn[l][None], w1[l].astype(BF16), w3[l].astype(BF16), w2[l].astype(BF16),
                  g_final[None], tm_post, final=(l == depth - 1))
    return x2.reshape(B, S, D)
```

```python
import functools
import math

import jax
import jax.numpy as jnp
import numpy as np
from jax import lax
from jax.experimental import pallas as pl
from jax.experimental.pallas import tpu as pltpu

F32 = jnp.float32
BF16 = jnp.bfloat16

D_MODEL = 1024
HEAD_DIM = 64
ROPE_THETA = 10000.0
EPS = 1e-6
MASK_VALUE = -1e30
LOG2E = 1.4426950408889634

MLA_HEADS = 8
MLA_Q_RANK = 256
MLA_KV_RANK = 128
MLA_NOPE_DIM = 64
MLA_ROPE_DIM = 32
MLA_V_DIM = 64
MLA_HEAD_PAD = 128
MLA_DEN_ROWS = 16
MLA_STEPS_PER_TRIP = 14

DIL_PATTERNS = ((128, 1), (512, 4), (2048, 16))
DIL_HEADS_PER_GROUP = 4
DIL_HEADS = DIL_HEADS_PER_GROUP * len(DIL_PATTERNS)
DIL_WIDTH = DIL_HEADS * HEAD_DIM
DIL_GROUP_WIDTH = DIL_HEADS_PER_GROUP * HEAD_DIM

NA_HEADS = 8
NA_KH = 8
NA_KW = 16
GRID_W = 64
NA_WIDTH = NA_HEADS * HEAD_DIM
NA_ROWS_PER_BLOCK = 8
NA_WINDOW_ROWS = 16

N_BRANCH = 3
D_FF = -(-(8 * D_MODEL) // (3 * 256)) * 256

LANE = 128

COL_C = 0
COL_QB = 512
COL_KB = COL_QB + DIL_WIDTH
COL_VB = COL_KB + DIL_WIDTH
COL_QBR = COL_VB + DIL_WIDTH
COL_KBR = COL_QBR + DIL_WIDTH
COL_QKVC = COL_KBR + DIL_WIDTH
COL_G = COL_QKVC + 3 * NA_WIDTH
COL_END = COL_G + N_BRANCH * D_MODEL

VMEM_LIMIT = 56 * 1024 * 1024


def _dot(a, b):
    return jnp.dot(a, b, preferred_element_type=F32)


def _dot_nt(a, b):
    return lax.dot_general(a, b, (((1,), (1,)), ((), ())), preferred_element_type=F32)


def _dot_tn(a, b):
    return lax.dot_general(a, b, (((0,), (0,)), ((), ())), preferred_element_type=F32)


def _rms(x, g):
    return x * lax.rsqrt(jnp.mean(x * x, axis=-1, keepdims=True) + EPS) * g


def _resident(shape):
    del shape
    return pl.BlockSpec(memory_space=pltpu.VMEM)


def _in_proj_kernel(x_ref, g_ref, w_ref, gq_ref, gkv_ref, wqT_ref, wqrT_ref, wk_ref, wvT_ref,
                    cqT_ref, sqT_ref, tk_ref, cbq_ref, sbq_ref, cbk_ref, sbk_ref,
                    qT_out, k_out, vT_out, b0_out, b1_out, b2_out, qkvc_out, gates_out,
                    stage_ref):
    tm = x_ref.shape[0]
    h = _rms(x_ref[...], g_ref[...]).astype(BF16)

    def proj(lo, hi):
        return _dot(h, w_ref[:, lo:hi])

    c = proj(COL_C, COL_QB)
    cqn = _rms(c[:, :MLA_Q_RANK], gq_ref[...]).astype(BF16)
    ckvn = _rms(c[:, MLA_Q_RANK:MLA_Q_RANK + MLA_KV_RANK], gkv_ref[...]).astype(BF16)
    kr = (c[:, MLA_Q_RANK + MLA_KV_RANK:] * tk_ref[...]).astype(BF16)
    k_out[...] = _dot(jnp.concatenate([ckvn, kr], axis=1), wk_ref[...]).astype(BF16)
    vT_out[0] = _dot_nt(wvT_ref[...], ckvn).astype(BF16)
    q_main = _dot_nt(wqT_ref[...], cqn)
    q_rot = _dot_nt(wqrT_ref[...], cqn)
    cq = cqT_ref[...]
    sq = sqT_ref[...]
    for hd in range(MLA_HEADS):
        rows = slice(hd * MLA_HEAD_PAD, (hd + 1) * MLA_HEAD_PAD)
        qT_out[0, rows, :] = (q_main[rows] * cq + q_rot[rows] * sq).astype(BF16)

    reps = DIL_WIDTH // LANE
    qb = proj(COL_QB, COL_KB) * jnp.tile(cbq_ref[...], (1, reps)) \
        + proj(COL_QBR, COL_KBR) * jnp.tile(sbq_ref[...], (1, reps))
    kb = proj(COL_KB, COL_VB) * jnp.tile(cbk_ref[...], (1, reps)) \
        + proj(COL_KBR, COL_QKVC) * jnp.tile(sbk_ref[...], (1, reps))
    vb = proj(COL_VB, COL_QBR)
    GW = DIL_GROUP_WIDTH
    for j, val in enumerate((qb, kb, vb)):
        b0_out[0, 0, :, j * GW:(j + 1) * GW] = val[:, :GW].astype(BF16)
    for j, val in enumerate((qb, kb, vb)):
        for cb in range(DIL_WIDTH // LANE):
            stage_ref[j, cb] = val[:, cb * LANE:(cb + 1) * LANE]
    for grp, out in ((1, b1_out), (2, b2_out)):
        d = DIL_PATTERNS[grp][1]
        for r in range(d):
            for j in range(3):
                for cb in range(GW // LANE):
                    tile = stage_ref.at[j, grp * (GW // LANE) + cb]
                    rows = tile[pl.ds(r, tm // d, stride=d), :]
                    lo = j * GW + cb * LANE
                    out[0, r, :, lo:lo + LANE] = rows.astype(BF16)

    qkvc_out[...] = proj(COL_QKVC, COL_G).astype(BF16)

    for br in range(N_BRANCH):
        lo = COL_G + br * D_MODEL
        z = proj(lo, lo + D_MODEL)
        gates_out[:, br * D_MODEL:(br + 1) * D_MODEL] = (1.0 / (1.0 + jnp.exp(-z))).astype(BF16)


def _in_proj(x2, g_mix, wp, g_q, g_kv, wqT, wqrT, wk, wvT, tabs, B, S, tm):
    T = B * S
    nS = S // tm
    cqT, sqT, tk, cbq, sbq, cbk, sbk = tabs
    row_tab = pl.BlockSpec((tm, LANE), lambda i: (i % nS, 0))
    col_tab = pl.BlockSpec((MLA_HEAD_PAD, tm), lambda i: (0, i % nS))
    in_specs = [
        pl.BlockSpec((tm, D_MODEL), lambda i: (i, 0)),
        _resident(None), _resident(None), _resident(None), _resident(None),
        _resident(None), _resident(None), _resident(None), _resident(None),
        col_tab, col_tab, row_tab, row_tab, row_tab, row_tab, row_tab,
    ]
    HP = MLA_HEADS * MLA_HEAD_PAD
    HV = MLA_HEADS * MLA_V_DIM
    out_shape = [
        jax.ShapeDtypeStruct((B, HP, S), BF16),
        jax.ShapeDtypeStruct((T, HP), BF16),
        jax.ShapeDtypeStruct((B, HV, S), BF16),
        *[jax.ShapeDtypeStruct((B, d, S // d, DIL_WIDTH), BF16) for _, d in DIL_PATTERNS],
        jax.ShapeDtypeStruct((T, 3 * NA_WIDTH), BF16),
        jax.ShapeDtypeStruct((T, N_BRANCH * D_MODEL), BF16),
    ]
    out_specs = [
        pl.BlockSpec((1, HP, tm), lambda i: (i // nS, 0, i % nS)),
        pl.BlockSpec((tm, HP), lambda i: (i, 0)),
        pl.BlockSpec((1, HV, tm), lambda i: (i // nS, 0, i % nS)),
        *[pl.BlockSpec((1, d, tm // d, DIL_WIDTH), lambda i: (i // nS, 0, i % nS, 0))
          for _, d in DIL_PATTERNS],
        pl.BlockSpec((tm, 3 * NA_WIDTH), lambda i: (i, 0)),
        pl.BlockSpec((tm, N_BRANCH * D_MODEL), lambda i: (i, 0)),
    ]
    return pl.pallas_call(
        _in_proj_kernel,
        grid=(T // tm,),
        in_specs=in_specs,
        out_specs=out_specs,
        out_shape=out_shape,
        scratch_shapes=[pltpu.VMEM((3, DIL_WIDTH // LANE, tm, LANE), F32)],
        compiler_params=pltpu.CompilerParams(
            dimension_semantics=("arbitrary",), vmem_limit_bytes=VMEM_LIMIT),
        name="in_proj",
    )(x2, g_mix, wp, g_q, g_kv, wqT, wqrT, wk, wvT, cqT, sqT, tk, cbq, sbq, cbk, sbk)


def _mla_kernel(qT_ref, k_ref, vT_ref, oT_ref, s0_ref, s1_ref, p0_ref, p1_ref, acc_ref, *,
                tk, n_chunks):
    tq = qT_ref.shape[2]
    s_slots = (s0_ref, s1_ref)
    p_slots = (p0_ref, p1_ref)

    def scores(c, slot):
        off = pl.multiple_of(c * tk, tk)
        s = _dot(k_ref[0, pl.ds(off, tk), :], qT_ref[0])
        s_slots[slot][...] = s
        cmax8 = jnp.max(s.reshape(tk // 8, 8, tq), axis=0)
        return jnp.max(cmax8, axis=0, keepdims=True)

    def softmax(slot, m, cmax):
        m_new = jnp.maximum(m, cmax)
        p = jnp.exp2(s_slots[slot][...] - m_new)
        p_slots[slot][...] = p.astype(BF16)
        return m_new, jnp.exp2(m - m_new)

    ones_rows = (lax.broadcasted_iota(jnp.int32, (MLA_DEN_ROWS, tk), 0) == 0).astype(BF16)

    def values(c, slot, alpha):
        off = pl.multiple_of(c * tk, tk)
        v = jnp.concatenate([vT_ref[0, :, pl.ds(off, tk)], ones_rows], axis=0)
        acc_ref[...] = alpha * acc_ref[...] + _dot(v, p_slots[slot][...])

    def step(c, parity, carry, with_scores=True):
        m, cmax, alpha_prev = carry
        cmax_next = scores(c + 1, 1 - parity) if with_scores else cmax
        m, alpha = softmax(parity, m, cmax)
        values(c - 1, 1 - parity, alpha_prev)
        return m, cmax_next, alpha

    assert n_chunks % 2 == 0 and n_chunks >= 4
    cmax0 = scores(0, 0)
    cmax1 = scores(1, 1)
    m, alpha = softmax(0, jnp.full((1, tq), -jnp.inf, F32), cmax0)
    acc_ref[...] = jnp.zeros_like(acc_ref)
    carry = (m, cmax1, alpha)

    trips = (n_chunks - 2) // MLA_STEPS_PER_TRIP

    def body(j, carry):
        c = MLA_STEPS_PER_TRIP * j + 1
        for u in range(MLA_STEPS_PER_TRIP):
            carry = step(c + u, (1 + u) % 2, carry)
        return carry

    carry = lax.fori_loop(0, trips, body, carry)
    last = n_chunks - 1
    for c in range(MLA_STEPS_PER_TRIP * trips + 1, last):
        carry = step(c, c % 2, carry)
    _, _, alpha = step(last, 1, carry, with_scores=False)
    values(last, 1, alpha)
    oT_ref[0] = (acc_ref[:MLA_V_DIM] / acc_ref[MLA_V_DIM:MLA_V_DIM + 1]).astype(BF16)


def _mla(qT, k, vT, B, S, tq, tk):
    grid = (B, MLA_HEADS, S // tq)
    return pl.pallas_call(
        functools.partial(_mla_kernel, tk=tk, n_chunks=S // tk),
        grid=grid,
        scratch_shapes=[pltpu.VMEM((tk, tq), F32), pltpu.VMEM((tk, tq), F32),
                        pltpu.VMEM((tk, tq), BF16), pltpu.VMEM((tk, tq), BF16),
                        pltpu.VMEM((MLA_V_DIM + MLA_DEN_ROWS, tq), F32)],
        in_specs=[
            pl.BlockSpec((1, MLA_HEAD_PAD, tq), lambda b, h, i: (b, h, i)),
            pl.BlockSpec((1, S, MLA_HEAD_PAD), lambda b, h, i: (b, 0, h)),
            pl.BlockSpec((1, MLA_V_DIM, S), lambda b, h, i: (b, h, 0)),
        ],
        out_specs=pl.BlockSpec((1, MLA_V_DIM, tq), lambda b, h, i: (b, h, i)),
        out_shape=jax.ShapeDtypeStruct((B, MLA_HEADS * MLA_V_DIM, S), BF16),
        compiler_params=pltpu.CompilerParams(
            dimension_semantics=("arbitrary", "arbitrary", "arbitrary"),
            vmem_limit_bytes=VMEM_LIMIT),
        name="mla",
    )(qT, k, vT)


def _head_lane_mask(width):
    return lax.broadcasted_iota(jnp.int32, (1, width), 1) < HEAD_DIM


def _dilated_kernel(q_ref, k_ref, v_ref, o_ref, lse_ref, *, tq, win, radius, L):
    i = pl.program_id(2)
    start = jnp.clip(i * tq - radius, 0, L - win)
    start = pl.multiple_of(start, radius)
    qpos = i * tq + lax.broadcasted_iota(jnp.int32, (tq, win), 0)
    kpos = start + lax.broadcasted_iota(jnp.int32, (tq, win), 1)
    band = jnp.abs(qpos - kpos) <= radius
    first = _head_lane_mask(LANE)
    for pair in range(DIL_HEADS_PER_GROUP // 2):
        lanes = slice(pair * LANE, (pair + 1) * LANE)
        qp = q_ref[:, lanes]
        kp = k_ref[pl.ds(start, win), lanes]
        vp = v_ref[pl.ds(start, win), lanes]
        o_pair = None
        lse_pair = None
        for hh in range(2):
            sel = first if hh == 0 else jnp.logical_not(first)
            qm = jnp.where(sel, qp, jnp.zeros_like(qp))
            vm = jnp.where(sel, vp, jnp.zeros_like(vp))
            s = jnp.where(band, _dot_nt(qm, kp), MASK_VALUE)
            m = jnp.max(s, axis=1, keepdims=True)
            p = jnp.exp2(s - m)
            l = jnp.sum(p, axis=1, keepdims=True)
            o = _dot(p.astype(BF16), vm) / l
            lse = m + jnp.log2(l)
            if hh == 0:
                o_pair = o
                lse_pair = lse
            else:
                o_pair = o_pair + o
                lse_pair = jnp.where(first, lse_pair, lse)
        o_ref[:, lanes] = o_pair.astype(BF16)
        lse_ref[:, lanes] = jnp.broadcast_to(lse_pair, (tq, LANE))


def _dilated_group(qkvb, B, S, group, dilation, radius):
    L = S // dilation
    tq = min(256, L)
    win = tq + 2 * radius
    assert L >= win and L % tq == 0
    GW = DIL_GROUP_WIDTH
    out_spec = pl.BlockSpec((None, None, tq, GW), lambda b, r, i: (b, r, i, 0))
    return pl.pallas_call(
        functools.partial(_dilated_kernel, tq=tq, win=win, radius=radius, L=L),
        grid=(B, dilation, L // tq),
        in_specs=[
            pl.BlockSpec((None, None, tq, GW), lambda b, r, i: (b, r, i, 0)),
            pl.BlockSpec((None, None, L, GW), lambda b, r, i: (b, r, 0, 1)),
            pl.BlockSpec((None, None, L, GW), lambda b, r, i: (b, r, 0, 2)),
        ],
        out_specs=[out_spec, out_spec],
        out_shape=[
            jax.ShapeDtypeStruct((B, dilation, L, GW), BF16),
            jax.ShapeDtypeStruct((B, dilation, L, GW), F32),
        ],
        compiler_params=pltpu.CompilerParams(
            dimension_semantics=("arbitrary", "arbitrary", "arbitrary"),
            vmem_limit_bytes=VMEM_LIMIT),
        name=f"dilated_g{group}",
    )(qkvb, qkvb, qkvb)


def _na_kernel(q_ref, k_ref, v_ref, bias_ref, o_ref, *, rows):
    i = pl.program_id(2)
    start_row = jnp.clip(i * NA_ROWS_PER_BLOCK - NA_KH // 2, 0, rows - NA_WINDOW_ROWS)
    start = pl.multiple_of(start_row * GRID_W, GRID_W)
    win = NA_WINDOW_ROWS * GRID_W
    qp = q_ref[0]
    kp = k_ref[0, pl.ds(start, win), :]
    vp = v_ref[0, pl.ds(start, win), :]
    first = _head_lane_mask(LANE)
    o_pair = None
    for hh in range(2):
        sel = first if hh == 0 else jnp.logical_not(first)
        qm = jnp.where(sel, qp, jnp.zeros_like(qp))
        vm = jnp.where(sel, vp, jnp.zeros_like(vp))
        s = _dot_nt(qm, kp) + bias_ref[0, hh]
        m = jnp.max(s, axis=1, keepdims=True)
        p = jnp.exp2(s - m)
        l = jnp.sum(p, axis=1, keepdims=True)
        o = _dot(p.astype(BF16), vm) / l
        o_pair = o if hh == 0 else o_pair + o
    o_ref[0] = o_pair.astype(BF16)


def _na_bias(rpb, rows):
    nb = rows // NA_ROWS_PER_BLOCK
    n_dr, n_dc = 2 * NA_KH - 1, 2 * NA_KW - 1
    qc = np.arange(GRID_W)[:, None]
    kc = np.arange(GRID_W)[None, :]
    cs = np.clip(qc - NA_KW // 2, 0, GRID_W - NA_KW)
    col_ok = (kc >= cs) & (kc < cs + NA_KW)
    dc = kc - qc + (NA_KW - 1)
    rpb2 = rpb.astype(F32) * LOG2E
    tc = sum(rpb2[:, :, j, None, None] * jnp.asarray((dc == j) & col_ok, F32) for j in range(n_dc))
    tc = jnp.where(jnp.asarray(col_ok), tc, MASK_VALUE)
    blank = jnp.full((NA_HEADS, 1, GRID_W, GRID_W), MASK_VALUE, F32)
    tcx = jnp.concatenate([blank, tc, blank], axis=1)
    pairs = jnp.concatenate([tcx[:, :-1], tcx[:, 1:]], axis=-1)
    plan = []
    for blk in (0, 1, nb - 1):
        R = blk * NA_ROWS_PER_BLOCK
        start = min(max(R - NA_KH // 2, 0), rows - NA_WINDOW_ROWS)
        per_row = []
        for a in range(NA_ROWS_PER_BLOCK):
            r0 = min(max(R + a - NA_KH // 2, 0), rows - NA_KH)
            ok = [r0 <= start + b < r0 + NA_KH for b in range(NA_WINDOW_ROWS)]
            ents = []
            for j in range(NA_WINDOW_ROWS // 2):
                dr0 = start + 2 * j - (R + a) + NA_KH - 1
                if not (ok[2 * j] or ok[2 * j + 1]):
                    ents.append(None)
                else:
                    assert 0 <= dr0 + 1 <= n_dr
                    ents.append((dr0 + 1, ok[2 * j], ok[2 * j + 1]))
            per_row.append(ents)
        plan.append(per_row)
    tq = NA_ROWS_PER_BLOCK * GRID_W
    win = NA_WINDOW_ROWS * GRID_W
    return pl.pallas_call(
        functools.partial(_na_bias_kernel, plan=plan),
        grid=(NA_HEADS,),
        in_specs=[pl.BlockSpec((1, n_dr + 1, GRID_W, 2 * GRID_W), lambda h: (h, 0, 0, 0))],
        out_specs=pl.BlockSpec((3, 1, tq, win), lambda h: (0, h, 0, 0)),
        out_shape=jax.ShapeDtypeStruct((3, NA_HEADS, tq, win), F32),
        compiler_params=pltpu.CompilerParams(
            dimension_semantics=("arbitrary",), vmem_limit_bytes=VMEM_LIMIT),
        name="na_bias",
    )(pairs)


def _na_bias_kernel(pairs_ref, out_ref, *, plan):
    left = lax.broadcasted_iota(jnp.int32, (GRID_W, 2 * GRID_W), 1) < GRID_W
    masked = jnp.full((GRID_W, 2 * GRID_W), MASK_VALUE, F32)
    for v, per_row in enumerate(plan):
        for a, ents in enumerate(per_row):
            for j, ent in enumerate(ents):
                piece = masked
                if ent is not None:
                    e, left_ok, right_ok = ent
                    piece = pairs_ref[0, e]
                    if not left_ok:
                        piece = jnp.where(left, masked, piece)
                    if not right_ok:
                        piece = jnp.where(left, piece, masked)
                out_ref[v, 0, a * GRID_W:(a + 1) * GRID_W, j * LANE:(j + 1) * LANE] = piece


def _na(qkvc, bias, B, S):
    rows = S // GRID_W
    nb = rows // NA_ROWS_PER_BLOCK
    assert nb >= 3
    tq = NA_ROWS_PER_BLOCK * GRID_W
    win = NA_WINDOW_ROWS * GRID_W
    npair = NA_HEADS // 2
    qv = qkvc.reshape(B, S, 3 * NA_WIDTH)

    def variant(i):
        return jnp.where(i == 0, 0, jnp.where(i == nb - 1, 2, 1))

    return pl.pallas_call(
        functools.partial(_na_kernel, rows=rows),
        grid=(B, npair, nb),
        in_specs=[
            pl.BlockSpec((1, tq, LANE), lambda b, p, i: (b, i, p)),
            pl.BlockSpec((1, S, LANE), lambda b, p, i: (b, 0, npair + p)),
            pl.BlockSpec((1, S, LANE), lambda b, p, i: (b, 0, 2 * npair + p)),
            pl.BlockSpec((1, 2, tq, win), lambda b, p, i: (variant(i), p, 0, 0)),
        ],
        out_specs=pl.BlockSpec((1, tq, LANE), lambda b, p, i: (b, i, p)),
        out_shape=jax.ShapeDtypeStruct((B, S, NA_WIDTH), BF16),
        compiler_params=pltpu.CompilerParams(
            dimension_semantics=("arbitrary", "arbitrary", "arbitrary"),
            vmem_limit_bytes=VMEM_LIMIT),
        name="na",
    )(qv, qv, qv, bias).reshape(B * S, NA_WIDTH)


def _merge_kernel(x_ref, yaT_ref, o1_ref, o2_ref, o3_ref, l1_ref, l2_ref, l3_ref, yc_ref, g_ref,
                  wpa_ref, wpb_ref, wpc_ref, wo_ref, out_ref, o2n_ref, o3n_ref, l2n_ref, l3n_ref):
    a = _dot_tn(yaT_ref[0], wpa_ref[...])
    def token_order(src, dst):
        d, n = src.shape[0], src.shape[1]
        for r in range(d):
            for cb in range(DIL_GROUP_WIDTH // LANE):
                dst.at[cb][pl.ds(r, n, stride=d), :] = (
                    src[r, :, cb * LANE:(cb + 1) * LANE].astype(F32))
        return jnp.concatenate([dst[cb] for cb in range(DIL_GROUP_WIDTH // LANE)], axis=1)

    o2, l2 = token_order(o2_ref, o2n_ref), token_order(l2_ref, l2n_ref)
    o3, l3 = token_order(o3_ref, o3n_ref), token_order(l3_ref, l3n_ref)
    l1 = l1_ref[...]
    mx = jnp.maximum(jnp.maximum(l1, l2), l3)
    w1, w2, w3 = jnp.exp2(l1 - mx), jnp.exp2(l2 - mx), jnp.exp2(l3 - mx)
    yb = (w1 * o1_ref[...].astype(F32) + w2 * o2 + w3 * o3) / (w1 + w2 + w3)
    bm = _dot(yb.astype(BF16), wpb_ref[...])
    cm = _dot(yc_ref[...], wpc_ref[...])
    D = D_MODEL
    merged = (g_ref[:, 0:D].astype(F32) * a + g_ref[:, D:2 * D].astype(F32) * bm
              + g_ref[:, 2 * D:3 * D].astype(F32) * cm)
    out_ref[...] = x_ref[...] + _dot(merged.astype(BF16), wo_ref[...])


def _merge(x2, yaT, obs, lses, yc, gates, wpa, wpb, wpc, wo, B, S, tm):
    T = B * S
    nS = S // tm
    HV = MLA_HEADS * MLA_V_DIM
    GW = DIL_GROUP_WIDTH
    row = lambda w: pl.BlockSpec((tm, w), lambda i: (i, 0))

    def grouped(d):
        if d == 1:
            return pl.BlockSpec((None, None, tm, GW), lambda i: (i // nS, 0, i % nS, 0))
        return pl.BlockSpec((None, d, tm // d, GW), lambda i: (i // nS, 0, i % nS, 0))

    group_specs = [grouped(d) for _, d in DIL_PATTERNS]
    return pl.pallas_call(
        _merge_kernel,
        grid=(T // tm,),
        in_specs=[
            row(D_MODEL),
            pl.BlockSpec((1, HV, tm), lambda i: (i // nS, 0, i % nS)),
            *group_specs, *group_specs,
            row(NA_WIDTH), row(N_BRANCH * D_MODEL),
            _resident(None), _resident(None), _resident(None), _resident(None),
        ],
        out_specs=row(D_MODEL),
        out_shape=jax.ShapeDtypeStruct((T, D_MODEL), F32),
        scratch_shapes=[pltpu.VMEM((GW // LANE, tm, LANE), F32)] * 4,
        compiler_params=pltpu.CompilerParams(
            dimension_semantics=("arbitrary",), vmem_limit_bytes=VMEM_LIMIT),
        name="merge",
    )(x2, yaT, *obs, *lses, yc, gates, wpa, wpb, wpc, wo)


def _ffn_kernel(x_ref, g_ref, w1_ref, w3_ref, w2_ref, gf_ref, out_ref, *, final):
    x = x_ref[...]
    h = _rms(x, g_ref[...]).astype(BF16)
    u = _dot(h, w1_ref[...])
    v = _dot(h, w3_ref[...])
    a = (u / (1.0 + jnp.exp(-u)) * v).astype(BF16)
    y = x + _dot(a, w2_ref[...])
    if final:
        y = _rms(y, gf_ref[...])
    out_ref[...] = y


def _ffn(x2, g_ffn, w1, w3, w2, g_final, tm, final):
    T = x2.shape[0]
    row = pl.BlockSpec((tm, D_MODEL), lambda i: (i, 0))
    return pl.pallas_call(
        functools.partial(_ffn_kernel, final=final),
        grid=(T // tm,),
        in_specs=[row, _resident(None), _resident(None), _resident(None), _resident(None),
                  _resident(None)],
        out_specs=row,
        out_shape=jax.ShapeDtypeStruct((T, D_MODEL), F32),
        compiler_params=pltpu.CompilerParams(
            dimension_semantics=("arbitrary",), vmem_limit_bytes=VMEM_LIMIT),
        name="ffn",
    )(x2, g_ffn, w1, w3, w2, g_final)


def _rot_half_cols(w, dim):
    k = w.shape[0]
    w3 = w.reshape(k, -1, dim)
    return jnp.concatenate([-w3[..., dim // 2:], w3[..., :dim // 2]], axis=-1).reshape(k, -1)


def _pack_layer(w_in, w_uq, w_ukv):
    c_q, c_kv, k_r, qkv_b, qkv_c, gate = jnp.split(
        w_in,
        (MLA_Q_RANK, MLA_Q_RANK + MLA_KV_RANK, MLA_Q_RANK + MLA_KV_RANK + MLA_ROPE_DIM,
         MLA_Q_RANK + MLA_KV_RANK + MLA_ROPE_DIM + 3 * DIL_WIDTH,
         MLA_Q_RANK + MLA_KV_RANK + MLA_ROPE_DIM + 3 * DIL_WIDTH + 3 * NA_WIDTH), axis=1)
    q_b, k_b, v_b = jnp.split(qkv_b, 3, axis=1)
    q_c, k_c, v_c = jnp.split(qkv_c, 3, axis=1)
    q_c = q_c * (HEAD_DIM ** -0.5 * LOG2E)
    pad = jnp.zeros((D_MODEL, COL_QB - (MLA_Q_RANK + MLA_KV_RANK + 2 * MLA_ROPE_DIM)), F32)
    wp = jnp.concatenate(
        [c_q, c_kv, k_r, _rot_half_cols(k_r, MLA_ROPE_DIM), pad,
         q_b, k_b, v_b, _rot_half_cols(q_b, HEAD_DIM), _rot_half_cols(k_b, HEAD_DIM),
         q_c, k_c, v_c, gate], axis=1).astype(BF16)
    assert wp.shape[1] == COL_END

    uq = w_uq.reshape(MLA_Q_RANK, MLA_HEADS, MLA_NOPE_DIM + MLA_ROPE_DIM)
    zpad = jnp.zeros((MLA_Q_RANK, MLA_HEADS, MLA_HEAD_PAD - MLA_NOPE_DIM - MLA_ROPE_DIM), F32)
    q_main = jnp.concatenate([uq, zpad], axis=-1)
    rope = uq[..., MLA_NOPE_DIM:]
    rope_rot = jnp.concatenate([-rope[..., MLA_ROPE_DIM // 2:], rope[..., :MLA_ROPE_DIM // 2]], -1)
    q_rot = jnp.concatenate([jnp.zeros_like(uq[..., :MLA_NOPE_DIM]), rope_rot, zpad], axis=-1)
    HP = MLA_HEADS * MLA_HEAD_PAD
    wqT = q_main.reshape(MLA_Q_RANK, HP).T.astype(BF16)
    wqrT = q_rot.reshape(MLA_Q_RANK, HP).T.astype(BF16)

    ukv = w_ukv.reshape(MLA_KV_RANK, MLA_HEADS, MLA_NOPE_DIM + MLA_V_DIM)
    k_nope = jnp.concatenate(
        [ukv[..., :MLA_NOPE_DIM],
         jnp.zeros((MLA_KV_RANK, MLA_HEADS, MLA_HEAD_PAD - MLA_NOPE_DIM), F32)], axis=-1)
    place = np.zeros((LANE, MLA_HEADS, MLA_HEAD_PAD), np.float32)
    for j in range(2 * MLA_ROPE_DIM):
        place[j, :, MLA_NOPE_DIM + j % MLA_ROPE_DIM] = 1.0
    wk = jnp.concatenate([k_nope.reshape(MLA_KV_RANK, HP), jnp.asarray(place).reshape(LANE, HP)],
                         axis=0).astype(BF16)
    wvT = ukv[..., MLA_NOPE_DIM:].reshape(MLA_KV_RANK, MLA_HEADS * MLA_V_DIM).T.astype(BF16)
    return wp, wqT, wqrT, wk, wvT


def _rope_cos_sin(seq_len, dim):
    pos = jnp.arange(seq_len, dtype=F32)
    inv = jnp.power(ROPE_THETA, -jnp.arange(0, dim, 2, dtype=F32) / dim)
    ang = pos[:, None] * inv[None, :]
    return jnp.cos(ang), jnp.sin(ang)


def _tables(S):
    cos_a, sin_a = _rope_cos_sin(S, MLA_ROPE_DIM)
    qs = (MLA_NOPE_DIM + MLA_ROPE_DIM) ** -0.5 * LOG2E
    ones = jnp.ones((S, MLA_NOPE_DIM), F32)
    zeros_n = jnp.zeros((S, MLA_NOPE_DIM), F32)
    zpad = jnp.zeros((S, MLA_HEAD_PAD - MLA_NOPE_DIM - MLA_ROPE_DIM), F32)
    cqT = (jnp.concatenate([ones, cos_a, cos_a, zpad], axis=1) * qs).T
    sqT = (jnp.concatenate([zeros_n, sin_a, sin_a, zpad], axis=1) * qs).T
    tk = jnp.concatenate([cos_a, cos_a, sin_a, sin_a, jnp.zeros((S, LANE - 2 * MLA_ROPE_DIM), F32)],
                         axis=1)
    cos_b, sin_b = _rope_cos_sin(S, HEAD_DIM)
    cb = jnp.tile(cos_b, (1, LANE // (HEAD_DIM // 2)))
    sb = jnp.tile(sin_b, (1, LANE // (HEAD_DIM // 2)))
    bs = HEAD_DIM ** -0.5 * LOG2E
    return cqT, sqT, tk, cb * bs, sb * bs, cb, sb


def kernel(x, w_in, g_mix, g_q, g_kv, w_uq, w_ukv, rpb, w_pa, w_pb, w_pc, w_o, g_ffn, w1, w3, w2,
           g_final):
    B, S, D = x.shape
    depth = w_in.shape[0]
    T = B * S
    tm_proj = 256
    tm_post = 512
    tabs = _tables(S)
    x2 = x.reshape(T, D)
    for l in range(depth):
        wp, wqT, wqrT, wk, wvT = _pack_layer(w_in[l], w_uq[l], w_ukv[l])
        qT, k, vT, qkvb0, qkvb1, qkvb2, qkvc, gates = _in_proj(
            x2, g_mix[l][None], wp, g_q[l][None], g_kv[l][None], wqT, wqrT, wk, wvT, tabs,
            B, S, tm_proj)
        yaT = _mla(qT, k.reshape(B, S, -1), vT, B, S, tq=512, tk=512)
        obs, lses = [], []
        for grp, (qkvb, (window, dilation)) in enumerate(zip((qkvb0, qkvb1, qkvb2), DIL_PATTERNS)):
            o, lse = _dilated_group(qkvb, B, S, grp, dilation, window // (2 * dilation))
            obs.append(o)
            lses.append(lse)
        yc = _na(qkvc, _na_bias(rpb[l], S // GRID_W), B, S)
        x2 = _merge(x2, yaT, obs, lses, yc, gates, w_pa[l].astype(BF16), w_pb[l].astype(BF16),
                    w_pc[l].astype(BF16), w_o[l].astype(BF16), B, S, tm_post)
        x2 = _ffn(x2, g_ffn[l][None], w1[l].astype(BF16), w3[l].astype(BF16), w2[l].astype(BF16),
                  g_final[None], tm_post, final=(l == depth - 1))
    return x2.reshape(B, S, D)
```

```python
import functools
import math

import jax
import jax.numpy as jnp
import numpy as np
from jax import lax
from jax.experimental import pallas as pl
from jax.experimental.pallas import tpu as pltpu

F32 = jnp.float32
BF16 = jnp.bfloat16

D_MODEL = 1024
HEAD_DIM = 64
ROPE_THETA = 10000.0
EPS = 1e-6
MASK_VALUE = -1e30
LOG2E = 1.4426950408889634

MLA_HEADS = 8
MLA_Q_RANK = 256
MLA_KV_RANK = 128
MLA_NOPE_DIM = 64
MLA_ROPE_DIM = 32
MLA_V_DIM = 64
MLA_HEAD_PAD = 128
MLA_DEN_ROWS = 16
MLA_REF_COL = MLA_NOPE_DIM + MLA_ROPE_DIM
MLA_REF_ROWS = 16

DIL_PATTERNS = ((128, 1), (512, 4), (2048, 16))
DIL_HEADS_PER_GROUP = 4
DIL_HEADS = DIL_HEADS_PER_GROUP * len(DIL_PATTERNS)
DIL_WIDTH = DIL_HEADS * HEAD_DIM
DIL_GROUP_WIDTH = DIL_HEADS_PER_GROUP * HEAD_DIM

NA_HEADS = 8
NA_KH = 8
NA_KW = 16
GRID_W = 64
NA_WIDTH = NA_HEADS * HEAD_DIM
NA_ROWS_PER_BLOCK = 4
NA_WINDOW_ROWS = 12

N_BRANCH = 3
D_FF = -(-(8 * D_MODEL) // (3 * 256)) * 256

LANE = 128

COL_C = 0
COL_QB = 512
COL_KB = COL_QB + DIL_WIDTH
COL_VB = COL_KB + DIL_WIDTH
COL_QKVC = COL_VB + DIL_WIDTH
COL_G = COL_QKVC + 3 * NA_WIDTH
COL_END = COL_G + N_BRANCH * D_MODEL

VMEM_LIMIT = 56 * 1024 * 1024


def _dot(a, b):
    return jnp.dot(a, b, preferred_element_type=F32)


def _dot_nt(a, b):
    return lax.dot_general(a, b, (((1,), (1,)), ((), ())), preferred_element_type=F32)


def _dot_tn(a, b):
    return lax.dot_general(a, b, (((0,), (0,)), ((), ())), preferred_element_type=F32)


def _rms(x, g):
    return x * lax.rsqrt(jnp.mean(x * x, axis=-1, keepdims=True) + EPS) * g


def _resident(shape):
    del shape
    return pl.BlockSpec(memory_space=pltpu.VMEM)


def _in_proj_kernel(x_ref, g_ref, w_ref, gq_ref, gkv_ref, wqT_ref, wqrT_ref, wk_ref, wvT_ref,
                    cqT_ref, sqT_ref, tk_ref, cbq_ref, sbq_ref, cbk_ref, sbk_ref,
                    qT_out, k_out, vT_out, b0_out, b1_out, b2_out, qkvc_out, gates_out,
                    stage_ref):
    tm = x_ref.shape[0]
    h = _rms(x_ref[...], g_ref[...]).astype(BF16)

    def proj(lo, hi):
        return _dot(h, w_ref[:, lo:hi])

    c = proj(COL_C, COL_QB)
    cqn = _rms(c[:, :MLA_Q_RANK], gq_ref[...]).astype(BF16)
    ckvn = _rms(c[:, MLA_Q_RANK:MLA_Q_RANK + MLA_KV_RANK], gkv_ref[...]).astype(BF16)
    kr = (c[:, MLA_Q_RANK + MLA_KV_RANK:] * tk_ref[...]).astype(BF16)
    lane = lax.broadcasted_iota(jnp.int32, (1, MLA_HEADS * MLA_HEAD_PAD), 1)
    ref_col = jnp.where((lane & (MLA_HEAD_PAD - 1)) == MLA_REF_COL, 1.0, 0.0)
    k_out[...] = (_dot(jnp.concatenate([ckvn, kr], axis=1), wk_ref[...]) + ref_col).astype(BF16)
    vT_out[0] = _dot_nt(wvT_ref[...], ckvn).astype(BF16)
    q_main = _dot_nt(wqT_ref[...], cqn)
    q_rot = _dot_nt(wqrT_ref[...], cqn)
    cq = cqT_ref[...]
    sq = sqT_ref[...]
    for hd in range(MLA_HEADS):
        rows = slice(hd * MLA_HEAD_PAD, (hd + 1) * MLA_HEAD_PAD)
        qT_out[0, rows, :] = (q_main[rows] * cq + q_rot[rows] * sq).astype(BF16)

    reps = DIL_WIDTH // LANE
    half = HEAD_DIM // 2
    lane = lax.broadcasted_iota(jnp.int32, (1, DIL_WIDTH), 1)
    in_first_half = (lane & (HEAD_DIM - 1)) < half

    def rope(x, cos_ref, sin_ref):
        swapped = jnp.where(in_first_half, pltpu.roll(x, DIL_WIDTH - half, axis=1),
                            pltpu.roll(x, half, axis=1))
        return x * jnp.tile(cos_ref[...], (1, reps)) + swapped * jnp.tile(sin_ref[...], (1, reps))

    qb = rope(proj(COL_QB, COL_KB), cbq_ref, sbq_ref)
    kb = rope(proj(COL_KB, COL_VB), cbk_ref, sbk_ref)
    vb = proj(COL_VB, COL_QKVC)
    GW = DIL_GROUP_WIDTH
    for j, val in enumerate((qb, kb, vb)):
        b0_out[0, 0, :, j * GW:(j + 1) * GW] = val[:, :GW].astype(BF16)
    for j, val in enumerate((qb, kb, vb)):
        for cb in range(DIL_WIDTH // LANE):
            stage_ref[j, cb] = val[:, cb * LANE:(cb + 1) * LANE]
    for grp, out in ((1, b1_out), (2, b2_out)):
        d = DIL_PATTERNS[grp][1]
        for r in range(d):
            for j in range(3):
                for cb in range(GW // LANE):
                    tile = stage_ref.at[j, grp * (GW // LANE) + cb]
                    rows = tile[pl.ds(r, tm // d, stride=d), :]
                    lo = j * GW + cb * LANE
                    out[0, r, :, lo:lo + LANE] = rows.astype(BF16)

    qkvc_out[...] = proj(COL_QKVC, COL_G).astype(BF16)

    for br in range(N_BRANCH):
        lo = COL_G + br * D_MODEL
        z = proj(lo, lo + D_MODEL)
        gates_out[:, br * D_MODEL:(br + 1) * D_MODEL] = (1.0 / (1.0 + jnp.exp(-z))).astype(BF16)


def _in_proj(x2, g_mix, wp, g_q, g_kv, wqT, wqrT, wk, wvT, tabs, B, S, tm):
    T = B * S
    nS = S // tm
    cqT, sqT, tk, cbq, sbq, cbk, sbk = tabs
    row_tab = pl.BlockSpec((tm, LANE), lambda i: (i % nS, 0))
    col_tab = pl.BlockSpec((MLA_HEAD_PAD, tm), lambda i: (0, i % nS))
    in_specs = [
        pl.BlockSpec((tm, D_MODEL), lambda i: (i, 0)),
        _resident(None), _resident(None), _resident(None), _resident(None),
        _resident(None), _resident(None), _resident(None), _resident(None),
        col_tab, col_tab, row_tab, row_tab, row_tab, row_tab, row_tab,
    ]
    HP = MLA_HEADS * MLA_HEAD_PAD
    HV = MLA_HEADS * MLA_V_DIM
    out_shape = [
        jax.ShapeDtypeStruct((B, HP, S), BF16),
        jax.ShapeDtypeStruct((T, HP), BF16),
        jax.ShapeDtypeStruct((B, HV, S), BF16),
        *[jax.ShapeDtypeStruct((B, d, S // d, DIL_WIDTH), BF16) for _, d in DIL_PATTERNS],
        jax.ShapeDtypeStruct((T, 3 * NA_WIDTH), BF16),
        jax.ShapeDtypeStruct((T, N_BRANCH * D_MODEL), BF16),
    ]
    out_specs = [
        pl.BlockSpec((1, HP, tm), lambda i: (i // nS, 0, i % nS)),
        pl.BlockSpec((tm, HP), lambda i: (i, 0)),
        pl.BlockSpec((1, HV, tm), lambda i: (i // nS, 0, i % nS)),
        *[pl.BlockSpec((1, d, tm // d, DIL_WIDTH), lambda i: (i // nS, 0, i % nS, 0))
          for _, d in DIL_PATTERNS],
        pl.BlockSpec((tm, 3 * NA_WIDTH), lambda i: (i, 0)),
        pl.BlockSpec((tm, N_BRANCH * D_MODEL), lambda i: (i, 0)),
    ]
    return pl.pallas_call(
        _in_proj_kernel,
        grid=(T // tm,),
        in_specs=in_specs,
        out_specs=out_specs,
        out_shape=out_shape,
        scratch_shapes=[pltpu.VMEM((3, DIL_WIDTH // LANE, tm, LANE), F32)],
        compiler_params=pltpu.CompilerParams(
            dimension_semantics=("arbitrary",), vmem_limit_bytes=VMEM_LIMIT),
        name="in_proj",
    )(x2, g_mix, wp, g_q, g_kv, wqT, wqrT, wk, wvT, cqT, sqT, tk, cbq, sbq, cbk, sbk)


def _denominator_rows(tk):
    return (lax.broadcasted_iota(jnp.int32, (MLA_DEN_ROWS, tk), 0) == 0).astype(BF16)


def _mla_online_path(qT_ref, k_ref, vT_ref, oT_ref, s_slots, p_slots, acc_ref, *, tk, n_chunks):
    tq = qT_ref.shape[2]

    def scores(c, slot):
        off = pl.multiple_of(c * tk, tk)
        s = _dot(k_ref[0, pl.ds(off, tk), :], qT_ref[0])
        s_slots[slot][...] = s
        return jnp.max(s, axis=0, keepdims=True)

    def softmax(slot, m, cmax):
        m_new = jnp.maximum(m, cmax)
        p = jnp.exp2(s_slots[slot][...] - m_new)
        p_slots[slot][...] = p.astype(BF16)
        return m_new, jnp.exp2(m - m_new)

    ones_rows = _denominator_rows(tk)

    def values(c, slot, alpha):
        off = pl.multiple_of(c * tk, tk)
        v = jnp.concatenate([vT_ref[0, :, pl.ds(off, tk)], ones_rows], axis=0)
        acc_ref[...] = alpha * acc_ref[...] + _dot(v, p_slots[slot][...])

    def step(c, parity, carry, with_scores=True):
        m, cmax, alpha_prev = carry
        cmax_next = scores(c + 1, 1 - parity) if with_scores else cmax
        m, alpha = softmax(parity, m, cmax)
        values(c - 1, 1 - parity, alpha_prev)
        return m, cmax_next, alpha

    assert n_chunks % 2 == 0 and n_chunks >= 4
    cmax0 = scores(0, 0)
    cmax1 = scores(1, 1)
    m, alpha = softmax(0, jnp.full((1, tq), -jnp.inf, F32), cmax0)
    acc_ref[...] = jnp.zeros_like(acc_ref)

    def body(j, carry):
        c = 2 * j + 1
        return step(c + 1, 0, step(c, 1, carry))

    carry = lax.fori_loop(0, n_chunks // 2 - 1, body, (m, cmax1, alpha))
    last = n_chunks - 1
    _, _, alpha = step(last, 1, carry, with_scores=False)
    values(last, 1, alpha)
    oT_ref[0] = (acc_ref[:MLA_V_DIM] / acc_ref[MLA_V_DIM:MLA_V_DIM + 1]).astype(BF16)


def _mla_kernel(qT_ref, k_ref, vT_ref, oT_ref, s0_ref, s1_ref, p0_ref, p1_ref, acc_ref, *,
                tk, n_chunks):
    q = qT_ref[0]
    tq = q.shape[1]
    head = _dot(k_ref[0, 0:MLA_REF_ROWS, :], q)
    ref = jnp.max(head, axis=0, keepdims=True)
    is_ref_row = lax.broadcasted_iota(jnp.int32, (MLA_REF_ROWS, tq), 0) == 0
    ref_rows = jnp.where(is_ref_row, -ref, 0.0).astype(BF16)
    q_aug = jnp.concatenate([q[:MLA_REF_COL], ref_rows, q[MLA_REF_COL + MLA_REF_ROWS:]], axis=0)
    ones_rows = _denominator_rows(tk)
    acc = jnp.zeros((MLA_V_DIM + MLA_DEN_ROWS, tq), F32)
    def shifted_scores(c):
        return _dot(k_ref[0, c * tk:(c + 1) * tk, :], q_aug)

    s = shifted_scores(0)
    for c in range(n_chunks):
        s_next = shifted_scores(c + 1) if c + 1 < n_chunks else None
        p = jnp.exp2(s).astype(BF16)
        v = jnp.concatenate([vT_ref[0, :, c * tk:(c + 1) * tk], ones_rows], axis=0)
        acc = acc + _dot(v, p)
        s = s_next
    finite = jnp.sum(acc * 0.0) == 0.0

    @pl.when(finite)
    def _():
        oT_ref[0] = (acc[:MLA_V_DIM] / acc[MLA_V_DIM:MLA_V_DIM + 1]).astype(BF16)

    @pl.when(jnp.logical_not(finite))
    def _():
        _mla_online_path(qT_ref, k_ref, vT_ref, oT_ref, (s0_ref, s1_ref), (p0_ref, p1_ref), acc_ref,
                         tk=tk, n_chunks=n_chunks)


def _mla(qT, k, vT, B, S, tq, tk):
    grid = (B, MLA_HEADS, S // tq)
    return pl.pallas_call(
        functools.partial(_mla_kernel, tk=tk, n_chunks=S // tk),
        grid=grid,
        scratch_shapes=[pltpu.VMEM((tk, tq), F32), pltpu.VMEM((tk, tq), F32),
                        pltpu.VMEM((tk, tq), BF16), pltpu.VMEM((tk, tq), BF16),
                        pltpu.VMEM((MLA_V_DIM + MLA_DEN_ROWS, tq), F32)],
        in_specs=[
            pl.BlockSpec((1, MLA_HEAD_PAD, tq), lambda b, h, i: (b, h, i)),
            pl.BlockSpec((1, S, MLA_HEAD_PAD), lambda b, h, i: (b, 0, h)),
            pl.BlockSpec((1, MLA_V_DIM, S), lambda b, h, i: (b, h, 0)),
        ],
        out_specs=pl.BlockSpec((1, MLA_V_DIM, tq), lambda b, h, i: (b, h, i)),
        out_shape=jax.ShapeDtypeStruct((B, MLA_HEADS * MLA_V_DIM, S), BF16),
        compiler_params=pltpu.CompilerParams(
            dimension_semantics=("arbitrary", "arbitrary", "arbitrary"),
            vmem_limit_bytes=VMEM_LIMIT),
        name="mla",
    )(qT, k, vT)


def _head_lane_mask(width):
    return lax.broadcasted_iota(jnp.int32, (1, width), 1) < HEAD_DIM


def _dilated_kernel(q_ref, k_ref, v_ref, o_ref, lse_ref, *, tq, win, radius, L):
    i = pl.program_id(2)
    start = jnp.clip(i * tq - radius, 0, L - win)
    start = pl.multiple_of(start, radius)
    qpos = i * tq + lax.broadcasted_iota(jnp.int32, (tq, win), 0)
    kpos = start + lax.broadcasted_iota(jnp.int32, (tq, win), 1)
    band = jnp.abs(qpos - kpos) <= radius
    first = _head_lane_mask(LANE)
    for pair in range(DIL_HEADS_PER_GROUP // 2):
        lanes = slice(pair * LANE, (pair + 1) * LANE)
        qp = q_ref[:, lanes]
        kp = k_ref[pl.ds(start, win), lanes]
        vp = v_ref[pl.ds(start, win), lanes]
        o_pair = None
        lse_pair = None
        for hh in range(2):
            sel = first if hh == 0 else jnp.logical_not(first)
            qm = jnp.where(sel, qp, jnp.zeros_like(qp))
            vm = jnp.where(sel, vp, jnp.zeros_like(vp))
            s = jnp.where(band, _dot_nt(qm, kp), MASK_VALUE)
            m = jnp.max(s, axis=1, keepdims=True)
            p = jnp.exp2(s - m)
            l = jnp.sum(p, axis=1, keepdims=True)
            o = _dot(p.astype(BF16), vm) / l
            lse = m + jnp.log2(l)
            if hh == 0:
                o_pair = o
                lse_pair = lse
            else:
                o_pair = o_pair + o
                lse_pair = jnp.where(first, lse_pair, lse)
        o_ref[:, lanes] = o_pair.astype(BF16)
        lse_ref[:, lanes] = jnp.broadcast_to(lse_pair, (tq, LANE))


def _dilated_group(qkvb, B, S, group, dilation, radius):
    L = S // dilation
    tq = min(256, L)
    win = tq + 2 * radius
    assert L >= win and L % tq == 0
    GW = DIL_GROUP_WIDTH
    out_spec = pl.BlockSpec((None, None, tq, GW), lambda b, r, i: (b, r, i, 0))
    return pl.pallas_call(
        functools.partial(_dilated_kernel, tq=tq, win=win, radius=radius, L=L),
        grid=(B, dilation, L // tq),
        in_specs=[
            pl.BlockSpec((None, None, tq, GW), lambda b, r, i: (b, r, i, 0)),
            pl.BlockSpec((None, None, L, GW), lambda b, r, i: (b, r, 0, 1)),
            pl.BlockSpec((None, None, L, GW), lambda b, r, i: (b, r, 0, 2)),
        ],
        out_specs=[out_spec, out_spec],
        out_shape=[
            jax.ShapeDtypeStruct((B, dilation, L, GW), BF16),
            jax.ShapeDtypeStruct((B, dilation, L, GW), F32),
        ],
        compiler_params=pltpu.CompilerParams(
            dimension_semantics=("arbitrary", "arbitrary", "arbitrary"),
            vmem_limit_bytes=VMEM_LIMIT),
        name=f"dilated_g{group}",
    )(qkvb, qkvb, qkvb)


def _na_kernel(q_ref, k_ref, v_ref, bias_ref, o_ref, *, rows):
    i = pl.program_id(2)
    start_row = jnp.clip(i * NA_ROWS_PER_BLOCK - NA_KH // 2, 0, rows - NA_WINDOW_ROWS)
    start = pl.multiple_of(start_row * GRID_W, GRID_W)
    win = NA_WINDOW_ROWS * GRID_W
    qp = q_ref[0]
    kp = k_ref[0, pl.ds(start, win), :]
    vp = v_ref[0, pl.ds(start, win), :]
    first = _head_lane_mask(LANE)
    o_pair = None
    for hh in range(2):
        sel = first if hh == 0 else jnp.logical_not(first)
        qm = jnp.where(sel, qp, jnp.zeros_like(qp))
        vm = jnp.where(sel, vp, jnp.zeros_like(vp))
        s = _dot_nt(qm, kp) + bias_ref[0, hh]
        m = jnp.max(s, axis=1, keepdims=True)
        p = jnp.exp2(s - m)
        l = jnp.sum(p, axis=1, keepdims=True)
        o = _dot(p.astype(BF16), vm) / l
        o_pair = o if hh == 0 else o_pair + o
    o_ref[0] = o_pair.astype(BF16)


def _na_bias(rpb, rows):
    nb = rows // NA_ROWS_PER_BLOCK
    n_dr, n_dc = 2 * NA_KH - 1, 2 * NA_KW - 1
    qc = np.arange(GRID_W)[:, None]
    kc = np.arange(GRID_W)[None, :]
    cs = np.clip(qc - NA_KW // 2, 0, GRID_W - NA_KW)
    col_ok = (kc >= cs) & (kc < cs + NA_KW)
    dc = kc - qc + (NA_KW - 1)
    rpb2 = rpb.astype(F32) * LOG2E
    tc = sum(rpb2[:, :, j, None, None] * jnp.asarray((dc == j) & col_ok, F32) for j in range(n_dc))
    tc = jnp.where(jnp.asarray(col_ok), tc, MASK_VALUE)
    blank = jnp.full((NA_HEADS, 1, GRID_W, GRID_W), MASK_VALUE, F32)
    tcx = jnp.concatenate([blank, tc, blank], axis=1)
    pairs = jnp.concatenate([tcx[:, :-1], tcx[:, 1:]], axis=-1)
    plan = []
    for blk in (0, 1, nb - 1):
        R = blk * NA_ROWS_PER_BLOCK
        start = min(max(R - NA_KH // 2, 0), rows - NA_WINDOW_ROWS)
        per_row = []
        for a in range(NA_ROWS_PER_BLOCK):
            r0 = min(max(R + a - NA_KH // 2, 0), rows - NA_KH)
            ok = [r0 <= start + b < r0 + NA_KH for b in range(NA_WINDOW_ROWS)]
            ents = []
            for j in range(NA_WINDOW_ROWS // 2):
                dr0 = start + 2 * j - (R + a) + NA_KH - 1
                if not (ok[2 * j] or ok[2 * j + 1]):
                    ents.append(None)
                else:
                    assert 0 <= dr0 + 1 <= n_dr
                    ents.append((dr0 + 1, ok[2 * j], ok[2 * j + 1]))
            per_row.append(ents)
        plan.append(per_row)
    tq = NA_ROWS_PER_BLOCK * GRID_W
    win = NA_WINDOW_ROWS * GRID_W
    return pl.pallas_call(
        functools.partial(_na_bias_kernel, plan=plan),
        grid=(NA_HEADS,),
        in_specs=[pl.BlockSpec((1, n_dr + 1, GRID_W, 2 * GRID_W), lambda h: (h, 0, 0, 0))],
        out_specs=pl.BlockSpec((3, 1, tq, win), lambda h: (0, h, 0, 0)),
        out_shape=jax.ShapeDtypeStruct((3, NA_HEADS, tq, win), F32),
        compiler_params=pltpu.CompilerParams(
            dimension_semantics=("arbitrary",), vmem_limit_bytes=VMEM_LIMIT),
        name="na_bias",
    )(pairs)


def _na_bias_kernel(pairs_ref, out_ref, *, plan):
    left = lax.broadcasted_iota(jnp.int32, (GRID_W, 2 * GRID_W), 1) < GRID_W
    masked = jnp.full((GRID_W, 2 * GRID_W), MASK_VALUE, F32)
    for v, per_row in enumerate(plan):
        for a, ents in enumerate(per_row):
            for j, ent in enumerate(ents):
                piece = masked
                if ent is not None:
                    e, left_ok, right_ok = ent
                    piece = pairs_ref[0, e]
                    if not left_ok:
                        piece = jnp.where(left, masked, piece)
                    if not right_ok:
                        piece = jnp.where(left, piece, masked)
                out_ref[v, 0, a * GRID_W:(a + 1) * GRID_W, j * LANE:(j + 1) * LANE] = piece


def _na(qkvc, bias, B, S):
    rows = S // GRID_W
    nb = rows // NA_ROWS_PER_BLOCK
    assert nb >= 3
    tq = NA_ROWS_PER_BLOCK * GRID_W
    win = NA_WINDOW_ROWS * GRID_W
    npair = NA_HEADS // 2
    qv = qkvc.reshape(B, S, 3 * NA_WIDTH)

    def variant(i):
        return jnp.where(i == 0, 0, jnp.where(i == nb - 1, 2, 1))

    return pl.pallas_call(
        functools.partial(_na_kernel, rows=rows),
        grid=(B, npair, nb),
        in_specs=[
            pl.BlockSpec((1, tq, LANE), lambda b, p, i: (b, i, p)),
            pl.BlockSpec((1, S, LANE), lambda b, p, i: (b, 0, npair + p)),
            pl.BlockSpec((1, S, LANE), lambda b, p, i: (b, 0, 2 * npair + p)),
            pl.BlockSpec((1, 2, tq, win), lambda b, p, i: (variant(i), p, 0, 0)),
        ],
        out_specs=pl.BlockSpec((1, tq, LANE), lambda b, p, i: (b, i, p)),
        out_shape=jax.ShapeDtypeStruct((B, S, NA_WIDTH), BF16),
        compiler_params=pltpu.CompilerParams(
            dimension_semantics=("arbitrary", "arbitrary", "arbitrary"),
            vmem_limit_bytes=VMEM_LIMIT),
        name="na",
    )(qv, qv, qv, bias).reshape(B * S, NA_WIDTH)


def _merge_kernel(x_ref, yaT_ref, o1_ref, o2_ref, o3_ref, l1_ref, l2_ref, l3_ref, yc_ref, g_ref,
                  wpa_ref, wpb_ref, wpc_ref, wo_ref, out_ref, o2n_ref, o3n_ref, l2n_ref, l3n_ref):
    a = _dot_tn(yaT_ref[0], wpa_ref[...])
    def token_order(src, dst):
        d, n = src.shape[0], src.shape[1]
        for r in range(d):
            for cb in range(DIL_GROUP_WIDTH // LANE):
                dst.at[cb][pl.ds(r, n, stride=d), :] = (
                    src[r, :, cb * LANE:(cb + 1) * LANE].astype(F32))
        return jnp.concatenate([dst[cb] for cb in range(DIL_GROUP_WIDTH // LANE)], axis=1)

    o2, l2 = token_order(o2_ref, o2n_ref), token_order(l2_ref, l2n_ref)
    o3, l3 = token_order(o3_ref, o3n_ref), token_order(l3_ref, l3n_ref)
    l1 = l1_ref[...]
    mx = jnp.maximum(jnp.maximum(l1, l2), l3)
    w1, w2, w3 = jnp.exp2(l1 - mx), jnp.exp2(l2 - mx), jnp.exp2(l3 - mx)
    yb = (w1 * o1_ref[...].astype(F32) + w2 * o2 + w3 * o3) / (w1 + w2 + w3)
    bm = _dot(yb.astype(BF16), wpb_ref[...])
    cm = _dot(yc_ref[...], wpc_ref[...])
    D = D_MODEL
    merged = (g_ref[:, 0:D].astype(F32) * a + g_ref[:, D:2 * D].astype(F32) * bm
              + g_ref[:, 2 * D:3 * D].astype(F32) * cm)
    out_ref[...] = x_ref[...] + _dot(merged.astype(BF16), wo_ref[...])


def _merge(x2, yaT, obs, lses, yc, gates, wpa, wpb, wpc, wo, B, S, tm):
    T = B * S
    nS = S // tm
    HV = MLA_HEADS * MLA_V_DIM
    GW = DIL_GROUP_WIDTH
    row = lambda w: pl.BlockSpec((tm, w), lambda i: (i, 0))

    def grouped(d):
        if d == 1:
            return pl.BlockSpec((None, None, tm, GW), lambda i: (i // nS, 0, i % nS, 0))
        return pl.BlockSpec((None, d, tm // d, GW), lambda i: (i // nS, 0, i % nS, 0))

    group_specs = [grouped(d) for _, d in DIL_PATTERNS]
    return pl.pallas_call(
        _merge_kernel,
        grid=(T // tm,),
        in_specs=[
            row(D_MODEL),
            pl.BlockSpec((1, HV, tm), lambda i: (i // nS, 0, i % nS)),
            *group_specs, *group_specs,
            row(NA_WIDTH), row(N_BRANCH * D_MODEL),
            _resident(None), _resident(None), _resident(None), _resident(None),
        ],
        out_specs=row(D_MODEL),
        out_shape=jax.ShapeDtypeStruct((T, D_MODEL), F32),
        scratch_shapes=[pltpu.VMEM((GW // LANE, tm, LANE), F32)] * 4,
        compiler_params=pltpu.CompilerParams(
            dimension_semantics=("arbitrary",), vmem_limit_bytes=VMEM_LIMIT),
        name="merge",
    )(x2, yaT, *obs, *lses, yc, gates, wpa, wpb, wpc, wo)


def _ffn_kernel(x_ref, g_ref, w1_ref, w3_ref, w2_ref, gf_ref, out_ref, *, final):
    x = x_ref[...]
    h = _rms(x, g_ref[...]).astype(BF16)
    u = _dot(h, w1_ref[...])
    v = _dot(h, w3_ref[...])
    a = (u / (1.0 + jnp.exp(-u)) * v).astype(BF16)
    y = x + _dot(a, w2_ref[...])
    if final:
        y = _rms(y, gf_ref[...])
    out_ref[...] = y


def _ffn(x2, g_ffn, w1, w3, w2, g_final, tm, final):
    T = x2.shape[0]
    row = pl.BlockSpec((tm, D_MODEL), lambda i: (i, 0))
    return pl.pallas_call(
        functools.partial(_ffn_kernel, final=final),
        grid=(T // tm,),
        in_specs=[row, _resident(None), _resident(None), _resident(None), _resident(None),
                  _resident(None)],
        out_specs=row,
        out_shape=jax.ShapeDtypeStruct((T, D_MODEL), F32),
        compiler_params=pltpu.CompilerParams(
            dimension_semantics=("arbitrary",), vmem_limit_bytes=VMEM_LIMIT),
        name="ffn",
    )(x2, g_ffn, w1, w3, w2, g_final)


def _rot_half_cols(w, dim):
    k = w.shape[0]
    w3 = w.reshape(k, -1, dim)
    return jnp.concatenate([-w3[..., dim // 2:], w3[..., :dim // 2]], axis=-1).reshape(k, -1)


def _pack_layer(w_in, w_uq, w_ukv):
    c_q, c_kv, k_r, qkv_b, qkv_c, gate = jnp.split(
        w_in,
        (MLA_Q_RANK, MLA_Q_RANK + MLA_KV_RANK, MLA_Q_RANK + MLA_KV_RANK + MLA_ROPE_DIM,
         MLA_Q_RANK + MLA_KV_RANK + MLA_ROPE_DIM + 3 * DIL_WIDTH,
         MLA_Q_RANK + MLA_KV_RANK + MLA_ROPE_DIM + 3 * DIL_WIDTH + 3 * NA_WIDTH), axis=1)
    q_b, k_b, v_b = jnp.split(qkv_b, 3, axis=1)
    q_c, k_c, v_c = jnp.split(qkv_c, 3, axis=1)
    q_c = q_c * (HEAD_DIM ** -0.5 * LOG2E)
    pad = jnp.zeros((D_MODEL, COL_QB - (MLA_Q_RANK + MLA_KV_RANK + 2 * MLA_ROPE_DIM)), F32)
    wp = jnp.concatenate(
        [c_q, c_kv, k_r, _rot_half_cols(k_r, MLA_ROPE_DIM), pad,
         q_b, k_b, v_b,
         q_c, k_c, v_c, gate], axis=1).astype(BF16)
    assert wp.shape[1] == COL_END

    uq = w_uq.reshape(MLA_Q_RANK, MLA_HEADS, MLA_NOPE_DIM + MLA_ROPE_DIM)
    zpad = jnp.zeros((MLA_Q_RANK, MLA_HEADS, MLA_HEAD_PAD - MLA_NOPE_DIM - MLA_ROPE_DIM), F32)
    q_main = jnp.concatenate([uq, zpad], axis=-1)
    rope = uq[..., MLA_NOPE_DIM:]
    rope_rot = jnp.concatenate([-rope[..., MLA_ROPE_DIM // 2:], rope[..., :MLA_ROPE_DIM // 2]], -1)
    q_rot = jnp.concatenate([jnp.zeros_like(uq[..., :MLA_NOPE_DIM]), rope_rot, zpad], axis=-1)
    HP = MLA_HEADS * MLA_HEAD_PAD
    wqT = q_main.reshape(MLA_Q_RANK, HP).T.astype(BF16)
    wqrT = q_rot.reshape(MLA_Q_RANK, HP).T.astype(BF16)

    ukv = w_ukv.reshape(MLA_KV_RANK, MLA_HEADS, MLA_NOPE_DIM + MLA_V_DIM)
    k_nope = jnp.concatenate(
        [ukv[..., :MLA_NOPE_DIM],
         jnp.zeros((MLA_KV_RANK, MLA_HEADS, MLA_HEAD_PAD - MLA_NOPE_DIM), F32)], axis=-1)
    place = np.zeros((LANE, MLA_HEADS, MLA_HEAD_PAD), np.float32)
    for j in range(2 * MLA_ROPE_DIM):
        place[j, :, MLA_NOPE_DIM + j % MLA_ROPE_DIM] = 1.0
    wk = jnp.concatenate([k_nope.reshape(MLA_KV_RANK, HP), jnp.asarray(place).reshape(LANE, HP)],
                         axis=0).astype(BF16)
    wvT = ukv[..., MLA_NOPE_DIM:].reshape(MLA_KV_RANK, MLA_HEADS * MLA_V_DIM).T.astype(BF16)
    return wp, wqT, wqrT, wk, wvT


def _rope_cos_sin(seq_len, dim):
    pos = jnp.arange(seq_len, dtype=F32)
    inv = jnp.power(ROPE_THETA, -jnp.arange(0, dim, 2, dtype=F32) / dim)
    ang = pos[:, None] * inv[None, :]
    return jnp.cos(ang), jnp.sin(ang)


def _tables(S):
    cos_a, sin_a = _rope_cos_sin(S, MLA_ROPE_DIM)
    qs = (MLA_NOPE_DIM + MLA_ROPE_DIM) ** -0.5 * LOG2E
    ones = jnp.ones((S, MLA_NOPE_DIM), F32)
    zeros_n = jnp.zeros((S, MLA_NOPE_DIM), F32)
    zpad = jnp.zeros((S, MLA_HEAD_PAD - MLA_NOPE_DIM - MLA_ROPE_DIM), F32)
    cqT = (jnp.concatenate([ones, cos_a, cos_a, zpad], axis=1) * qs).T
    sqT = (jnp.concatenate([zeros_n, sin_a, sin_a, zpad], axis=1) * qs).T
    tk = jnp.concatenate([cos_a, cos_a, sin_a, sin_a, jnp.zeros((S, LANE - 2 * MLA_ROPE_DIM), F32)],
                         axis=1)
    cos_b, sin_b = _rope_cos_sin(S, HEAD_DIM)
    cb = jnp.tile(cos_b, (1, LANE // (HEAD_DIM // 2)))
    sb = jnp.tile(jnp.concatenate([-sin_b, sin_b], axis=1), (1, LANE // HEAD_DIM))
    bs = HEAD_DIM ** -0.5 * LOG2E
    return cqT, sqT, tk, cb * bs, sb * bs, cb, sb


def kernel(x, w_in, g_mix, g_q, g_kv, w_uq, w_ukv, rpb, w_pa, w_pb, w_pc, w_o, g_ffn, w1, w3, w2,
           g_final):
    B, S, D = x.shape
    depth = w_in.shape[0]
    T = B * S
    tm_proj = 256
    tm_post = 512
    tabs = _tables(S)
    x2 = x.reshape(T, D)
    for l in range(depth):
        wp, wqT, wqrT, wk, wvT = _pack_layer(w_in[l], w_uq[l], w_ukv[l])
        qT, k, vT, qkvb0, qkvb1, qkvb2, qkvc, gates = _in_proj(
            x2, g_mix[l][None], wp, g_q[l][None], g_kv[l][None], wqT, wqrT, wk, wvT, tabs,
            B, S, tm_proj)
        yaT = _mla(qT, k.reshape(B, S, -1), vT, B, S, tq=512, tk=512)
        obs, lses = [], []
        for grp, (qkvb, (window, dilation)) in enumerate(zip((qkvb0, qkvb1, qkvb2), DIL_PATTERNS)):
            o, lse = _dilated_group(qkvb, B, S, grp, dilation, window // (2 * dilation))
            obs.append(o)
            lses.append(lse)
        yc = _na(qkvc, _na_bias(rpb[l], S // GRID_W), B, S)
        x2 = _merge(x2, yaT, obs, lses, yc, gates, w_pa[l].astype(BF16), w_pb[l].astype(BF16),
                    w_pc[l].astype(BF16), w_o[l].astype(BF16), B, S, tm_post)
        x2 = _ffn(x2, g_ffn[l][None], w1[l].astype(BF16), w3[l].astype(BF16), w2[l].astype(BF16),
                  g_final[None], tm_post, final=(l == depth - 1))
    return x2.reshape(B, S, D)
```

```python
import functools
import math

import jax
import jax.numpy as jnp
import numpy as np
from jax import lax
from jax.experimental import pallas as pl
from jax.experimental.pallas import tpu as pltpu

F32 = jnp.float32
BF16 = jnp.bfloat16

D_MODEL = 1024
HEAD_DIM = 64
ROPE_THETA = 10000.0
EPS = 1e-6
MASK_VALUE = -1e30
LOG2E = 1.4426950408889634

MLA_HEADS = 8
MLA_Q_RANK = 256
MLA_KV_RANK = 128
MLA_NOPE_DIM = 64
MLA_ROPE_DIM = 32
MLA_V_DIM = 64
MLA_HEAD_PAD = 128
MLA_DEN_ROWS = 16
MLA_REF_COL = MLA_NOPE_DIM + MLA_ROPE_DIM
MLA_REF_ROWS = 16

DIL_PATTERNS = ((128, 1), (512, 4), (2048, 16))
DIL_HEADS_PER_GROUP = 4
DIL_HEADS = DIL_HEADS_PER_GROUP * len(DIL_PATTERNS)
DIL_WIDTH = DIL_HEADS * HEAD_DIM
DIL_GROUP_WIDTH = DIL_HEADS_PER_GROUP * HEAD_DIM
DIL_TILES_PER_STEP = 2

NA_HEADS = 8
NA_KH = 8
NA_KW = 16
GRID_W = 64
NA_WIDTH = NA_HEADS * HEAD_DIM
NA_HEADS_PER_STEP = 4
NA_ROWS_PER_BLOCK = 4
NA_WINDOW_ROWS = 12

N_BRANCH = 3
D_FF = -(-(8 * D_MODEL) // (3 * 256)) * 256

LANE = 128

COL_C = 0
COL_QB = 512
COL_KB = COL_QB + DIL_WIDTH
COL_VB = COL_KB + DIL_WIDTH
COL_QKVC = COL_VB + DIL_WIDTH
COL_G = COL_QKVC + 3 * NA_WIDTH
COL_END = COL_G + N_BRANCH * D_MODEL

VMEM_LIMIT = 56 * 1024 * 1024


def _dot(a, b):
    return jnp.dot(a, b, preferred_element_type=F32)


def _dot_nt(a, b):
    return lax.dot_general(a, b, (((1,), (1,)), ((), ())), preferred_element_type=F32)


def _dot_tn(a, b):
    return lax.dot_general(a, b, (((0,), (0,)), ((), ())), preferred_element_type=F32)


def _rms(x, g):
    return x * lax.rsqrt(jnp.mean(x * x, axis=-1, keepdims=True) + EPS) * g


def _resident(shape):
    del shape
    return pl.BlockSpec(memory_space=pltpu.VMEM)


def _in_proj_kernel(x_ref, g_ref, w_ref, gq_ref, gkv_ref, wqT_ref, wqrT_ref, wk_ref, wvT_ref,
                    cqT_ref, sqT_ref, tk_ref, cbq_ref, sbq_ref, cbk_ref, sbk_ref,
                    qT_out, k_out, vT_out, b0_out, b1_out, b2_out, qkvc_out, gates_out,
                    stage_ref):
    tm = x_ref.shape[0]
    h = _rms(x_ref[...], g_ref[...]).astype(BF16)

    def proj(lo, hi):
        return _dot(h, w_ref[:, lo:hi])

    c = proj(COL_C, COL_QB)
    cqn = _rms(c[:, :MLA_Q_RANK], gq_ref[...]).astype(BF16)
    ckvn = _rms(c[:, MLA_Q_RANK:MLA_Q_RANK + MLA_KV_RANK], gkv_ref[...]).astype(BF16)
    kr = (c[:, MLA_Q_RANK + MLA_KV_RANK:] * tk_ref[...]).astype(BF16)
    lane = lax.broadcasted_iota(jnp.int32, (1, MLA_HEADS * MLA_HEAD_PAD), 1)
    ref_col = jnp.where((lane & (MLA_HEAD_PAD - 1)) == MLA_REF_COL, 1.0, 0.0)
    k_out[...] = (_dot(jnp.concatenate([ckvn, kr], axis=1), wk_ref[...]) + ref_col).astype(BF16)
    vT_out[0] = _dot_nt(wvT_ref[...], ckvn).astype(BF16)
    q_main = _dot_nt(wqT_ref[...], cqn)
    q_rot = _dot_nt(wqrT_ref[...], cqn)
    cq = cqT_ref[...]
    sq = sqT_ref[...]
    for hd in range(MLA_HEADS):
        rows = slice(hd * MLA_HEAD_PAD, (hd + 1) * MLA_HEAD_PAD)
        qT_out[0, rows, :] = (q_main[rows] * cq + q_rot[rows] * sq).astype(BF16)

    reps = DIL_WIDTH // LANE
    half = HEAD_DIM // 2
    lane = lax.broadcasted_iota(jnp.int32, (1, DIL_WIDTH), 1)
    in_first_half = (lane & (HEAD_DIM - 1)) < half

    def rope(x, cos_ref, sin_ref):
        swapped = jnp.where(in_first_half, pltpu.roll(x, DIL_WIDTH - half, axis=1),
                            pltpu.roll(x, half, axis=1))
        return x * jnp.tile(cos_ref[...], (1, reps)) + swapped * jnp.tile(sin_ref[...], (1, reps))

    qb = rope(proj(COL_QB, COL_KB), cbq_ref, sbq_ref)
    kb = rope(proj(COL_KB, COL_VB), cbk_ref, sbk_ref)
    vb = proj(COL_VB, COL_QKVC)
    GW = DIL_GROUP_WIDTH
    for j, val in enumerate((qb, kb, vb)):
        b0_out[0, 0, :, j * GW:(j + 1) * GW] = val[:, :GW].astype(BF16)
    for j, val in enumerate((qb, kb, vb)):
        for cb in range(DIL_WIDTH // LANE):
            stage_ref[j, cb] = val[:, cb * LANE:(cb + 1) * LANE]
    for grp, out in ((1, b1_out), (2, b2_out)):
        d = DIL_PATTERNS[grp][1]
        for r in range(d):
            for j in range(3):
                for cb in range(GW // LANE):
                    tile = stage_ref.at[j, grp * (GW // LANE) + cb]
                    rows = tile[pl.ds(r, tm // d, stride=d), :]
                    lo = j * GW + cb * LANE
                    out[0, r, :, lo:lo + LANE] = rows.astype(BF16)

    qkvc_out[...] = proj(COL_QKVC, COL_G).astype(BF16)

    for br in range(N_BRANCH):
        lo = COL_G + br * D_MODEL
        z = proj(lo, lo + D_MODEL)
        gates_out[:, br * D_MODEL:(br + 1) * D_MODEL] = (1.0 / (1.0 + jnp.exp(-z))).astype(BF16)


def _in_proj(x2, g_mix, wp, g_q, g_kv, wqT, wqrT, wk, wvT, tabs, B, S, tm):
    T = B * S
    nS = S // tm
    cqT, sqT, tk, cbq, sbq, cbk, sbk = tabs
    row_tab = pl.BlockSpec((tm, LANE), lambda i: (i % nS, 0))
    col_tab = pl.BlockSpec((MLA_HEAD_PAD, tm), lambda i: (0, i % nS))
    in_specs = [
        pl.BlockSpec((tm, D_MODEL), lambda i: (i, 0)),
        _resident(None), _resident(None), _resident(None), _resident(None),
        _resident(None), _resident(None), _resident(None), _resident(None),
        col_tab, col_tab, row_tab, row_tab, row_tab, row_tab, row_tab,
    ]
    HP = MLA_HEADS * MLA_HEAD_PAD
    HV = MLA_HEADS * MLA_V_DIM
    out_shape = [
        jax.ShapeDtypeStruct((B, HP, S), BF16),
        jax.ShapeDtypeStruct((T, HP), BF16),
        jax.ShapeDtypeStruct((B, HV, S), BF16),
        *[jax.ShapeDtypeStruct((B, d, S // d, DIL_WIDTH), BF16) for _, d in DIL_PATTERNS],
        jax.ShapeDtypeStruct((T, 3 * NA_WIDTH), BF16),
        jax.ShapeDtypeStruct((T, N_BRANCH * D_MODEL), BF16),
    ]
    out_specs = [
        pl.BlockSpec((1, HP, tm), lambda i: (i // nS, 0, i % nS)),
        pl.BlockSpec((tm, HP), lambda i: (i, 0)),
        pl.BlockSpec((1, HV, tm), lambda i: (i // nS, 0, i % nS)),
        *[pl.BlockSpec((1, d, tm // d, DIL_WIDTH), lambda i: (i // nS, 0, i % nS, 0))
          for _, d in DIL_PATTERNS],
        pl.BlockSpec((tm, 3 * NA_WIDTH), lambda i: (i, 0)),
        pl.BlockSpec((tm, N_BRANCH * D_MODEL), lambda i: (i, 0)),
    ]
    return pl.pallas_call(
        _in_proj_kernel,
        grid=(T // tm,),
        in_specs=in_specs,
        out_specs=out_specs,
        out_shape=out_shape,
        scratch_shapes=[pltpu.VMEM((3, DIL_WIDTH // LANE, tm, LANE), F32)],
        compiler_params=pltpu.CompilerParams(
            dimension_semantics=("arbitrary",), vmem_limit_bytes=VMEM_LIMIT),
        name="in_proj",
    )(x2, g_mix, wp, g_q, g_kv, wqT, wqrT, wk, wvT, cqT, sqT, tk, cbq, sbq, cbk, sbk)


def _denominator_rows(tk):
    return (lax.broadcasted_iota(jnp.int32, (MLA_DEN_ROWS, tk), 0) == 0).astype(BF16)


def _mla_online_path(q_view, k_ref, vT_ref, o_view, s_slots, p_slots, acc_ref, *, tk, n_chunks):
    tq = q_view.shape[1]

    def scores(c, slot):
        off = pl.multiple_of(c * tk, tk)
        s = _dot(k_ref[0, pl.ds(off, tk), :], q_view[...])
        s_slots[slot][...] = s
        return jnp.max(s, axis=0, keepdims=True)

    def softmax(slot, m, cmax):
        m_new = jnp.maximum(m, cmax)
        p = jnp.exp2(s_slots[slot][...] - m_new)
        p_slots[slot][...] = p.astype(BF16)
        return m_new, jnp.exp2(m - m_new)

    ones_rows = _denominator_rows(tk)

    def values(c, slot, alpha):
        off = pl.multiple_of(c * tk, tk)
        v = jnp.concatenate([vT_ref[0, :, pl.ds(off, tk)], ones_rows], axis=0)
        acc_ref[...] = alpha * acc_ref[...] + _dot(v, p_slots[slot][...])

    def step(c, parity, carry, with_scores=True):
        m, cmax, alpha_prev = carry
        cmax_next = scores(c + 1, 1 - parity) if with_scores else cmax
        m, alpha = softmax(parity, m, cmax)
        values(c - 1, 1 - parity, alpha_prev)
        return m, cmax_next, alpha

    assert n_chunks % 2 == 0 and n_chunks >= 4
    cmax0 = scores(0, 0)
    cmax1 = scores(1, 1)
    m, alpha = softmax(0, jnp.full((1, tq), -jnp.inf, F32), cmax0)
    acc_ref[...] = jnp.zeros_like(acc_ref)

    def body(j, carry):
        c = 2 * j + 1
        return step(c + 1, 0, step(c, 1, carry))

    carry = lax.fori_loop(0, n_chunks // 2 - 1, body, (m, cmax1, alpha))
    last = n_chunks - 1
    _, _, alpha = step(last, 1, carry, with_scores=False)
    values(last, 1, alpha)
    o_view[...] = (acc_ref[:MLA_V_DIM] / acc_ref[MLA_V_DIM:MLA_V_DIM + 1]).astype(BF16)


def _mla_kernel(qT_ref, k_ref, vT_ref, oT_ref, s0_ref, s1_ref, p0_ref, p1_ref, acc_ref, *,
                tq, tk, n_chunks):
    def tile(j, carry):
        off = pl.multiple_of(j * tq, tq)
        _mla_tile(qT_ref.at[0, :, pl.ds(off, tq)], k_ref, vT_ref, oT_ref.at[0, :, pl.ds(off, tq)],
                  (s0_ref, s1_ref), (p0_ref, p1_ref), acc_ref, tk=tk, n_chunks=n_chunks)
        return carry

    lax.fori_loop(0, qT_ref.shape[2] // tq, tile, 0)


def _mla_tile(q_view, k_ref, vT_ref, o_view, s_slots, p_slots, acc_ref, *, tk, n_chunks):
    q = q_view[...]
    tq = q.shape[1]
    head = _dot(k_ref[0, 0:MLA_REF_ROWS, :], q)
    ref = jnp.max(head, axis=0, keepdims=True)
    is_ref_row = lax.broadcasted_iota(jnp.int32, (MLA_REF_ROWS, tq), 0) == 0
    ref_rows = jnp.where(is_ref_row, -ref, 0.0).astype(BF16)
    q_aug = jnp.concatenate([q[:MLA_REF_COL], ref_rows, q[MLA_REF_COL + MLA_REF_ROWS:]], axis=0)
    ones_rows = _denominator_rows(tk)
    acc = jnp.zeros((MLA_V_DIM + MLA_DEN_ROWS, tq), F32)
    def shifted_scores(c):
        return _dot(k_ref[0, c * tk:(c + 1) * tk, :], q_aug)

    s = shifted_scores(0)
    for c in range(n_chunks):
        s_next = shifted_scores(c + 1) if c + 1 < n_chunks else None
        p = jnp.exp2(s).astype(BF16)
        v = jnp.concatenate([vT_ref[0, :, c * tk:(c + 1) * tk], ones_rows], axis=0)
        acc = acc + _dot(v, p)
        s = s_next
    finite = jnp.sum(acc * 0.0) == 0.0

    @pl.when(finite)
    def _():
        o_view[...] = (acc[:MLA_V_DIM] / acc[MLA_V_DIM:MLA_V_DIM + 1]).astype(BF16)

    @pl.when(jnp.logical_not(finite))
    def _():
        _mla_online_path(q_view, k_ref, vT_ref, o_view, s_slots, p_slots, acc_ref,
                         tk=tk, n_chunks=n_chunks)


def _mla(qT, k, vT, B, S, tq, tk, tiles_per_step):
    tb = tq * tiles_per_step
    assert S % tb == 0
    grid = (B, MLA_HEADS, S // tb)
    return pl.pallas_call(
        functools.partial(_mla_kernel, tq=tq, tk=tk, n_chunks=S // tk),
        grid=grid,
        scratch_shapes=[pltpu.VMEM((tk, tq), F32), pltpu.VMEM((tk, tq), F32),
                        pltpu.VMEM((tk, tq), BF16), pltpu.VMEM((tk, tq), BF16),
                        pltpu.VMEM((MLA_V_DIM + MLA_DEN_ROWS, tq), F32)],
        in_specs=[
            pl.BlockSpec((1, MLA_HEAD_PAD, tb), lambda b, h, i: (b, h, i)),
            pl.BlockSpec((1, S, MLA_HEAD_PAD), lambda b, h, i: (b, 0, h)),
            pl.BlockSpec((1, MLA_V_DIM, S), lambda b, h, i: (b, h, 0)),
        ],
        out_specs=pl.BlockSpec((1, MLA_V_DIM, tb), lambda b, h, i: (b, h, i)),
        out_shape=jax.ShapeDtypeStruct((B, MLA_HEADS * MLA_V_DIM, S), BF16),
        compiler_params=pltpu.CompilerParams(
            dimension_semantics=("arbitrary", "arbitrary", "arbitrary"),
            vmem_limit_bytes=VMEM_LIMIT),
        name="mla",
    )(qT, k, vT)


def _head_lane_mask(width):
    return lax.broadcasted_iota(jnp.int32, (1, width), 1) < HEAD_DIM


def _dilated_kernel(q_ref, k_ref, v_ref, o_ref, lse_ref, *, tq, win, radius, L):
    first = _head_lane_mask(LANE)
    for sub in range(q_ref.shape[0] // tq):
        tile = pl.program_id(2) * (q_ref.shape[0] // tq) + sub
        rows = slice(sub * tq, (sub + 1) * tq)
        start = jnp.clip(tile * tq - radius, 0, L - win)
        start = pl.multiple_of(start, radius)
        qpos = tile * tq + lax.broadcasted_iota(jnp.int32, (tq, win), 0)
        kpos = start + lax.broadcasted_iota(jnp.int32, (tq, win), 1)
        band = jnp.abs(qpos - kpos) <= radius
        for pair in range(DIL_HEADS_PER_GROUP // 2):
            lanes = slice(pair * LANE, (pair + 1) * LANE)
            qp = q_ref[rows, lanes]
            kp = k_ref[pl.ds(start, win), lanes]
            vp = v_ref[pl.ds(start, win), lanes]
            o_pair = None
            lse_pair = None
            for hh in range(2):
                sel = first if hh == 0 else jnp.logical_not(first)
                qm = jnp.where(sel, qp, jnp.zeros_like(qp))
                vm = jnp.where(sel, vp, jnp.zeros_like(vp))
                s = jnp.where(band, _dot_nt(qm, kp), MASK_VALUE)
                m = jnp.max(s, axis=1, keepdims=True)
                p = jnp.exp2(s - m)
                l = jnp.sum(p, axis=1, keepdims=True)
                o = _dot(p.astype(BF16), vm) / l
                lse = m + jnp.log2(l)
                if hh == 0:
                    o_pair = o
                    lse_pair = lse
                else:
                    o_pair = o_pair + o
                    lse_pair = jnp.where(first, lse_pair, lse)
            o_ref[rows, lanes] = o_pair.astype(BF16)
            lse_ref[rows, lanes] = jnp.broadcast_to(lse_pair, (tq, LANE))


def _dilated_group(qkvb, B, S, group, dilation, radius):
    L = S // dilation
    tq = min(256, L)
    win = tq + 2 * radius
    tb = tq * DIL_TILES_PER_STEP
    assert L >= win and L % tb == 0
    GW = DIL_GROUP_WIDTH
    out_spec = pl.BlockSpec((None, None, tb, GW), lambda b, r, i: (b, r, i, 0))
    return pl.pallas_call(
        functools.partial(_dilated_kernel, tq=tq, win=win, radius=radius, L=L),
        grid=(B, dilation, L // tb),
        in_specs=[
            pl.BlockSpec((None, None, tb, GW), lambda b, r, i: (b, r, i, 0)),
            pl.BlockSpec((None, None, L, GW), lambda b, r, i: (b, r, 0, 1)),
            pl.BlockSpec((None, None, L, GW), lambda b, r, i: (b, r, 0, 2)),
        ],
        out_specs=[out_spec, out_spec],
        out_shape=[
            jax.ShapeDtypeStruct((B, dilation, L, GW), BF16),
            jax.ShapeDtypeStruct((B, dilation, L, GW), F32),
        ],
        compiler_params=pltpu.CompilerParams(
            dimension_semantics=("arbitrary", "arbitrary", "arbitrary"),
            vmem_limit_bytes=VMEM_LIMIT),
        name=f"dilated_g{group}",
    )(qkvb, qkvb, qkvb)


def _na_kernel(q_ref, k_ref, v_ref, bias_ref, o_ref, *, rows):
    i = pl.program_id(2)
    start_row = jnp.clip(i * NA_ROWS_PER_BLOCK - NA_KH // 2, 0, rows - NA_WINDOW_ROWS)
    start = pl.multiple_of(start_row * GRID_W, GRID_W)
    win = NA_WINDOW_ROWS * GRID_W
    first = _head_lane_mask(LANE)
    for pp in range(q_ref.shape[2] // LANE):
        lanes = slice(pp * LANE, (pp + 1) * LANE)
        qp = q_ref[0, :, lanes]
        kp = k_ref[0, pl.ds(start, win), lanes]
        vp = v_ref[0, pl.ds(start, win), lanes]
        o_pair = None
        for hh in range(2):
            sel = first if hh == 0 else jnp.logical_not(first)
            qm = jnp.where(sel, qp, jnp.zeros_like(qp))
            vm = jnp.where(sel, vp, jnp.zeros_like(vp))
            s = _dot_nt(qm, kp) + bias_ref[0, 2 * pp + hh]
            m = jnp.max(s, axis=1, keepdims=True)
            p = jnp.exp2(s - m)
            l = jnp.sum(p, axis=1, keepdims=True)
            o = _dot(p.astype(BF16), vm) / l
            o_pair = o if hh == 0 else o_pair + o
        o_ref[0, :, lanes] = o_pair.astype(BF16)


def _na_bias(rpb, rows):
    nb = rows // NA_ROWS_PER_BLOCK
    n_dr, n_dc = 2 * NA_KH - 1, 2 * NA_KW - 1
    qc = np.arange(GRID_W)[:, None]
    kc = np.arange(GRID_W)[None, :]
    cs = np.clip(qc - NA_KW // 2, 0, GRID_W - NA_KW)
    col_ok = (kc >= cs) & (kc < cs + NA_KW)
    dc = kc - qc + (NA_KW - 1)
    rpb2 = rpb.astype(F32) * LOG2E
    tc = sum(rpb2[:, :, j, None, None] * jnp.asarray((dc == j) & col_ok, F32) for j in range(n_dc))
    tc = jnp.where(jnp.asarray(col_ok), tc, MASK_VALUE)
    blank = jnp.full((NA_HEADS, 1, GRID_W, GRID_W), MASK_VALUE, F32)
    tcx = jnp.concatenate([blank, tc, blank], axis=1)
    pairs = jnp.concatenate([tcx[:, :-1], tcx[:, 1:]], axis=-1)
    plan = []
    for blk in (0, 1, nb - 1):
        R = blk * NA_ROWS_PER_BLOCK
        start = min(max(R - NA_KH // 2, 0), rows - NA_WINDOW_ROWS)
        per_row = []
        for a in range(NA_ROWS_PER_BLOCK):
            r0 = min(max(R + a - NA_KH // 2, 0), rows - NA_KH)
            ok = [r0 <= start + b < r0 + NA_KH for b in range(NA_WINDOW_ROWS)]
            ents = []
            for j in range(NA_WINDOW_ROWS // 2):
                dr0 = start + 2 * j - (R + a) + NA_KH - 1
                if not (ok[2 * j] or ok[2 * j + 1]):
                    ents.append(None)
                else:
                    assert 0 <= dr0 + 1 <= n_dr
                    ents.append((dr0 + 1, ok[2 * j], ok[2 * j + 1]))
            per_row.append(ents)
        plan.append(per_row)
    tq = NA_ROWS_PER_BLOCK * GRID_W
    win = NA_WINDOW_ROWS * GRID_W
    return pl.pallas_call(
        functools.partial(_na_bias_kernel, plan=plan),
        grid=(NA_HEADS,),
        in_specs=[pl.BlockSpec((1, n_dr + 1, GRID_W, 2 * GRID_W), lambda h: (h, 0, 0, 0))],
        out_specs=pl.BlockSpec((3, 1, tq, win), lambda h: (0, h, 0, 0)),
        out_shape=jax.ShapeDtypeStruct((3, NA_HEADS, tq, win), F32),
        compiler_params=pltpu.CompilerParams(
            dimension_semantics=("arbitrary",), vmem_limit_bytes=VMEM_LIMIT),
        name="na_bias",
    )(pairs)


def _na_bias_kernel(pairs_ref, out_ref, *, plan):
    left = lax.broadcasted_iota(jnp.int32, (GRID_W, 2 * GRID_W), 1) < GRID_W
    masked = jnp.full((GRID_W, 2 * GRID_W), MASK_VALUE, F32)
    for v, per_row in enumerate(plan):
        for a, ents in enumerate(per_row):
            for j, ent in enumerate(ents):
                piece = masked
                if ent is not None:
                    e, left_ok, right_ok = ent
                    piece = pairs_ref[0, e]
                    if not left_ok:
                        piece = jnp.where(left, masked, piece)
                    if not right_ok:
                        piece = jnp.where(left, piece, masked)
                out_ref[v, 0, a * GRID_W:(a + 1) * GRID_W, j * LANE:(j + 1) * LANE] = piece


def _na(qkvc, bias, B, S):
    rows = S // GRID_W
    nb = rows // NA_ROWS_PER_BLOCK
    assert nb >= 3
    tq = NA_ROWS_PER_BLOCK * GRID_W
    win = NA_WINDOW_ROWS * GRID_W
    hps = NA_HEADS_PER_STEP
    width = hps * HEAD_DIM
    ngroup = NA_HEADS // hps
    qv = qkvc.reshape(B, S, 3 * NA_WIDTH)

    def variant(i):
        return jnp.where(i == 0, 0, jnp.where(i == nb - 1, 2, 1))

    return pl.pallas_call(
        functools.partial(_na_kernel, rows=rows),
        grid=(B, ngroup, nb),
        in_specs=[
            pl.BlockSpec((1, tq, width), lambda b, p, i: (b, i, p)),
            pl.BlockSpec((1, S, width), lambda b, p, i: (b, 0, ngroup + p)),
            pl.BlockSpec((1, S, width), lambda b, p, i: (b, 0, 2 * ngroup + p)),
            pl.BlockSpec((1, hps, tq, win), lambda b, p, i: (variant(i), p, 0, 0)),
        ],
        out_specs=pl.BlockSpec((1, tq, width), lambda b, p, i: (b, i, p)),
        out_shape=jax.ShapeDtypeStruct((B, S, NA_WIDTH), BF16),
        compiler_params=pltpu.CompilerParams(
            dimension_semantics=("arbitrary", "arbitrary", "arbitrary"),
            vmem_limit_bytes=VMEM_LIMIT),
        name="na",
    )(qv, qv, qv, bias).reshape(B * S, NA_WIDTH)


def _merge_kernel(x_ref, yaT_ref, o1_ref, o2_ref, o3_ref, l1_ref, l2_ref, l3_ref, yc_ref, g_ref,
                  wpa_ref, wpb_ref, wpc_ref, wo_ref, out_ref, o2n_ref, o3n_ref, l2n_ref, l3n_ref):
    a = _dot_tn(yaT_ref[0], wpa_ref[...])
    def token_order(src, dst):
        d, n = src.shape[0], src.shape[1]
        for r in range(d):
            for cb in range(DIL_GROUP_WIDTH // LANE):
                dst.at[cb][pl.ds(r, n, stride=d), :] = (
                    src[r, :, cb * LANE:(cb + 1) * LANE].astype(F32))
        return jnp.concatenate([dst[cb] for cb in range(DIL_GROUP_WIDTH // LANE)], axis=1)

    o2, l2 = token_order(o2_ref, o2n_ref), token_order(l2_ref, l2n_ref)
    o3, l3 = token_order(o3_ref, o3n_ref), token_order(l3_ref, l3n_ref)
    l1 = l1_ref[...]
    mx = jnp.maximum(jnp.maximum(l1, l2), l3)
    w1, w2, w3 = jnp.exp2(l1 - mx), jnp.exp2(l2 - mx), jnp.exp2(l3 - mx)
    yb = (w1 * o1_ref[...].astype(F32) + w2 * o2 + w3 * o3) / (w1 + w2 + w3)
    bm = _dot(yb.astype(BF16), wpb_ref[...])
    cm = _dot(yc_ref[...], wpc_ref[...])
    D = D_MODEL
    merged = (g_ref[:, 0:D].astype(F32) * a + g_ref[:, D:2 * D].astype(F32) * bm
              + g_ref[:, 2 * D:3 * D].astype(F32) * cm)
    out_ref[...] = x_ref[...] + _dot(merged.astype(BF16), wo_ref[...])


def _merge(x2, yaT, obs, lses, yc, gates, wpa, wpb, wpc, wo, B, S, tm):
    T = B * S
    nS = S // tm
    HV = MLA_HEADS * MLA_V_DIM
    GW = DIL_GROUP_WIDTH
    row = lambda w: pl.BlockSpec((tm, w), lambda i: (i, 0))

    def grouped(d):
        if d == 1:
            return pl.BlockSpec((None, None, tm, GW), lambda i: (i // nS, 0, i % nS, 0))
        return pl.BlockSpec((None, d, tm // d, GW), lambda i: (i // nS, 0, i % nS, 0))

    group_specs = [grouped(d) for _, d in DIL_PATTERNS]
    return pl.pallas_call(
        _merge_kernel,
        grid=(T // tm,),
        in_specs=[
            row(D_MODEL),
            pl.BlockSpec((1, HV, tm), lambda i: (i // nS, 0, i % nS)),
            *group_specs, *group_specs,
            row(NA_WIDTH), row(N_BRANCH * D_MODEL),
            _resident(None), _resident(None), _resident(None), _resident(None),
        ],
        out_specs=row(D_MODEL),
        out_shape=jax.ShapeDtypeStruct((T, D_MODEL), F32),
        scratch_shapes=[pltpu.VMEM((GW // LANE, tm, LANE), F32)] * 4,
        compiler_params=pltpu.CompilerParams(
            dimension_semantics=("arbitrary",), vmem_limit_bytes=VMEM_LIMIT),
        name="merge",
    )(x2, yaT, *obs, *lses, yc, gates, wpa, wpb, wpc, wo)


def _ffn_kernel(x_ref, g_ref, w1_ref, w3_ref, w2_ref, gf_ref, out_ref, *, final):
    x = x_ref[...]
    h = _rms(x, g_ref[...]).astype(BF16)
    u = _dot(h, w1_ref[...])
    v = _dot(h, w3_ref[...])
    a = (u / (1.0 + jnp.exp(-u)) * v).astype(BF16)
    y = x + _dot(a, w2_ref[...])
    if final:
        y = _rms(y, gf_ref[...])
    out_ref[...] = y


def _ffn(x2, g_ffn, w1, w3, w2, g_final, tm, final):
    T = x2.shape[0]
    row = pl.BlockSpec((tm, D_MODEL), lambda i: (i, 0))
    return pl.pallas_call(
        functools.partial(_ffn_kernel, final=final),
        grid=(T // tm,),
        in_specs=[row, _resident(None), _resident(None), _resident(None), _resident(None),
                  _resident(None)],
        out_specs=row,
        out_shape=jax.ShapeDtypeStruct((T, D_MODEL), F32),
        compiler_params=pltpu.CompilerParams(
            dimension_semantics=("arbitrary",), vmem_limit_bytes=VMEM_LIMIT),
        name="ffn",
    )(x2, g_ffn, w1, w3, w2, g_final)


def _rot_half_cols(w, dim):
    k = w.shape[0]
    w3 = w.reshape(k, -1, dim)
    return jnp.concatenate([-w3[..., dim // 2:], w3[..., :dim // 2]], axis=-1).reshape(k, -1)


def _pack_layer(w_in, w_uq, w_ukv):
    c_q, c_kv, k_r, qkv_b, qkv_c, gate = jnp.split(
        w_in,
        (MLA_Q_RANK, MLA_Q_RANK + MLA_KV_RANK, MLA_Q_RANK + MLA_KV_RANK + MLA_ROPE_DIM,
         MLA_Q_RANK + MLA_KV_RANK + MLA_ROPE_DIM + 3 * DIL_WIDTH,
         MLA_Q_RANK + MLA_KV_RANK + MLA_ROPE_DIM + 3 * DIL_WIDTH + 3 * NA_WIDTH), axis=1)
    q_b, k_b, v_b = jnp.split(qkv_b, 3, axis=1)
    q_c, k_c, v_c = jnp.split(qkv_c, 3, axis=1)
    q_c = q_c * (HEAD_DIM ** -0.5 * LOG2E)
    pad = jnp.zeros((D_MODEL, COL_QB - (MLA_Q_RANK + MLA_KV_RANK + 2 * MLA_ROPE_DIM)), F32)
    wp = jnp.concatenate(
        [c_q, c_kv, k_r, _rot_half_cols(k_r, MLA_ROPE_DIM), pad,
         q_b, k_b, v_b,
         q_c, k_c, v_c, gate], axis=1).astype(BF16)
    assert wp.shape[1] == COL_END

    uq = w_uq.reshape(MLA_Q_RANK, MLA_HEADS, MLA_NOPE_DIM + MLA_ROPE_DIM)
    zpad = jnp.zeros((MLA_Q_RANK, MLA_HEADS, MLA_HEAD_PAD - MLA_NOPE_DIM - MLA_ROPE_DIM), F32)
    q_main = jnp.concatenate([uq, zpad], axis=-1)
    rope = uq[..., MLA_NOPE_DIM:]
    rope_rot = jnp.concatenate([-rope[..., MLA_ROPE_DIM // 2:], rope[..., :MLA_ROPE_DIM // 2]], -1)
    q_rot = jnp.concatenate([jnp.zeros_like(uq[..., :MLA_NOPE_DIM]), rope_rot, zpad], axis=-1)
    HP = MLA_HEADS * MLA_HEAD_PAD
    wqT = q_main.reshape(MLA_Q_RANK, HP).T.astype(BF16)
    wqrT = q_rot.reshape(MLA_Q_RANK, HP).T.astype(BF16)

    ukv = w_ukv.reshape(MLA_KV_RANK, MLA_HEADS, MLA_NOPE_DIM + MLA_V_DIM)
    k_nope = jnp.concatenate(
        [ukv[..., :MLA_NOPE_DIM],
         jnp.zeros((MLA_KV_RANK, MLA_HEADS, MLA_HEAD_PAD - MLA_NOPE_DIM), F32)], axis=-1)
    place = np.zeros((LANE, MLA_HEADS, MLA_HEAD_PAD), np.float32)
    for j in range(2 * MLA_ROPE_DIM):
        place[j, :, MLA_NOPE_DIM + j % MLA_ROPE_DIM] = 1.0
    wk = jnp.concatenate([k_nope.reshape(MLA_KV_RANK, HP), jnp.asarray(place).reshape(LANE, HP)],
                         axis=0).astype(BF16)
    wvT = ukv[..., MLA_NOPE_DIM:].reshape(MLA_KV_RANK, MLA_HEADS * MLA_V_DIM).T.astype(BF16)
    return wp, wqT, wqrT, wk, wvT


def _rope_cos_sin(seq_len, dim):
    pos = jnp.arange(seq_len, dtype=F32)
    inv = jnp.power(ROPE_THETA, -jnp.arange(0, dim, 2, dtype=F32) / dim)
    ang = pos[:, None] * inv[None, :]
    return jnp.cos(ang), jnp.sin(ang)


def _tables(S):
    cos_a, sin_a = _rope_cos_sin(S, MLA_ROPE_DIM)
    qs = (MLA_NOPE_DIM + MLA_ROPE_DIM) ** -0.5 * LOG2E
    ones = jnp.ones((S, MLA_NOPE_DIM), F32)
    zeros_n = jnp.zeros((S, MLA_NOPE_DIM), F32)
    zpad = jnp.zeros((S, MLA_HEAD_PAD - MLA_NOPE_DIM - MLA_ROPE_DIM), F32)
    cqT = (jnp.concatenate([ones, cos_a, cos_a, zpad], axis=1) * qs).T
    sqT = (jnp.concatenate([zeros_n, sin_a, sin_a, zpad], axis=1) * qs).T
    tk = jnp.concatenate([cos_a, cos_a, sin_a, sin_a, jnp.zeros((S, LANE - 2 * MLA_ROPE_DIM), F32)],
                         axis=1)
    cos_b, sin_b = _rope_cos_sin(S, HEAD_DIM)
    cb = jnp.tile(cos_b, (1, LANE // (HEAD_DIM // 2)))
    sb = jnp.tile(jnp.concatenate([-sin_b, sin_b], axis=1), (1, LANE // HEAD_DIM))
    bs = HEAD_DIM ** -0.5 * LOG2E
    return cqT, sqT, tk, cb * bs, sb * bs, cb, sb


def kernel(x, w_in, g_mix, g_q, g_kv, w_uq, w_ukv, rpb, w_pa, w_pb, w_pc, w_o, g_ffn, w1, w3, w2,
           g_final):
    B, S, D = x.shape
    depth = w_in.shape[0]
    T = B * S
    tm_proj = 256
    tm_post = 512
    tabs = _tables(S)
    x2 = x.reshape(T, D)
    for l in range(depth):
        wp, wqT, wqrT, wk, wvT = _pack_layer(w_in[l], w_uq[l], w_ukv[l])
        qT, k, vT, qkvb0, qkvb1, qkvb2, qkvc, gates = _in_proj(
            x2, g_mix[l][None], wp, g_q[l][None], g_kv[l][None], wqT, wqrT, wk, wvT, tabs,
            B, S, tm_proj)
        yaT = _mla(qT, k.reshape(B, S, -1), vT, B, S, tq=512, tk=512, tiles_per_step=4)
        obs, lses = [], []
        for grp, (qkvb, (window, dilation)) in enumerate(zip((qkvb0, qkvb1, qkvb2), DIL_PATTERNS)):
            o, lse = _dilated_group(qkvb, B, S, grp, dilation, window // (2 * dilation))
            obs.append(o)
            lses.append(lse)
        yc = _na(qkvc, _na_bias(rpb[l], S // GRID_W), B, S)
        x2 = _merge(x2, yaT, obs, lses, yc, gates, w_pa[l].astype(BF16), w_pb[l].astype(BF16),
                    w_pc[l].astype(BF16), w_o[l].astype(BF16), B, S, tm_post)
        x2 = _ffn(x2, g_ffn[l][None], w1[l].astype(BF16), w3[l].astype(BF16), w2[l].astype(BF16),
                  g_final[None], tm_post, final=(l == depth - 1))
    return x2.reshape(B, S, D)
```

```python
import functools
import math

import jax
import jax.numpy as jnp
import numpy as np
from jax import lax
from jax.experimental import pallas as pl
from jax.experimental.pallas import tpu as pltpu

F32 = jnp.float32
BF16 = jnp.bfloat16

D_MODEL = 1024
HEAD_DIM = 64
ROPE_THETA = 10000.0
EPS = 1e-6
MASK_VALUE = -1e30
LOG2E = 1.4426950408889634

MLA_HEADS = 8
MLA_Q_RANK = 256
MLA_KV_RANK = 128
MLA_NOPE_DIM = 64
MLA_ROPE_DIM = 32
MLA_V_DIM = 64
MLA_HEAD_PAD = 128
MLA_DEN_ROWS = 16
MLA_REF_COL = MLA_NOPE_DIM + MLA_ROPE_DIM
MLA_REF_ROWS = 16

DIL_PATTERNS = ((128, 1), (512, 4), (2048, 16))
DIL_HEADS_PER_GROUP = 4
DIL_HEADS = DIL_HEADS_PER_GROUP * len(DIL_PATTERNS)
DIL_WIDTH = DIL_HEADS * HEAD_DIM
DIL_GROUP_WIDTH = DIL_HEADS_PER_GROUP * HEAD_DIM
DIL_TILES_PER_STEP = 4

NA_HEADS = 8
NA_KH = 8
NA_KW = 16
GRID_W = 64
NA_WIDTH = NA_HEADS * HEAD_DIM
NA_HEADS_PER_STEP = 4
NA_BLOCKS_PER_STEP = 2
NA_ROWS_PER_BLOCK = 4
NA_WINDOW_ROWS = 12

N_BRANCH = 3
D_FF = -(-(8 * D_MODEL) // (3 * 256)) * 256

LANE = 128

COL_C = 0
COL_QB = 512
COL_KB = COL_QB + DIL_WIDTH
COL_VB = COL_KB + DIL_WIDTH
COL_QKVC = COL_VB + DIL_WIDTH
COL_G = COL_QKVC + 3 * NA_WIDTH
COL_END = COL_G + N_BRANCH * D_MODEL

VMEM_LIMIT = 56 * 1024 * 1024


def _dot(a, b):
    return jnp.dot(a, b, preferred_element_type=F32)


def _dot_nt(a, b):
    return lax.dot_general(a, b, (((1,), (1,)), ((), ())), preferred_element_type=F32)


def _dot_tn(a, b):
    return lax.dot_general(a, b, (((0,), (0,)), ((), ())), preferred_element_type=F32)


def _rms(x, g):
    return x * lax.rsqrt(jnp.mean(x * x, axis=-1, keepdims=True) + EPS) * g


def _resident(shape):
    del shape
    return pl.BlockSpec(memory_space=pltpu.VMEM)


def _in_proj_kernel(x_ref, g_ref, w_ref, gq_ref, gkv_ref, wqT_ref, wqrT_ref, wk_ref, wvT_ref,
                    cqT_ref, sqT_ref, tk_ref, cbq_ref, sbq_ref, cbk_ref, sbk_ref,
                    qT_out, k_out, vT_out, b0_out, b1_out, b2_out, qkvc_out, gates_out,
                    stage_ref):
    tm = x_ref.shape[0]
    h = _rms(x_ref[...], g_ref[...]).astype(BF16)

    def proj(lo, hi):
        return _dot(h, w_ref[:, lo:hi])

    c = proj(COL_C, COL_QB)
    cqn = _rms(c[:, :MLA_Q_RANK], gq_ref[...]).astype(BF16)
    ckvn = _rms(c[:, MLA_Q_RANK:MLA_Q_RANK + MLA_KV_RANK], gkv_ref[...]).astype(BF16)
    kr = (c[:, MLA_Q_RANK + MLA_KV_RANK:] * tk_ref[...]).astype(BF16)
    lane = lax.broadcasted_iota(jnp.int32, (1, MLA_HEADS * MLA_HEAD_PAD), 1)
    ref_col = jnp.where((lane & (MLA_HEAD_PAD - 1)) == MLA_REF_COL, 1.0, 0.0)
    k_out[...] = (_dot(jnp.concatenate([ckvn, kr], axis=1), wk_ref[...]) + ref_col).astype(BF16)
    vT_out[0] = _dot_nt(wvT_ref[...], ckvn).astype(BF16)
    q_main = _dot_nt(wqT_ref[...], cqn)
    q_rot = _dot_nt(wqrT_ref[...], cqn)
    cq = cqT_ref[...]
    sq = sqT_ref[...]
    for hd in range(MLA_HEADS):
        rows = slice(hd * MLA_HEAD_PAD, (hd + 1) * MLA_HEAD_PAD)
        qT_out[0, rows, :] = (q_main[rows] * cq + q_rot[rows] * sq).astype(BF16)

    reps = DIL_WIDTH // LANE
    half = HEAD_DIM // 2
    lane = lax.broadcasted_iota(jnp.int32, (1, DIL_WIDTH), 1)
    in_first_half = (lane & (HEAD_DIM - 1)) < half

    def rope(x, cos_ref, sin_ref):
        swapped = jnp.where(in_first_half, pltpu.roll(x, DIL_WIDTH - half, axis=1),
                            pltpu.roll(x, half, axis=1))
        return x * jnp.tile(cos_ref[...], (1, reps)) + swapped * jnp.tile(sin_ref[...], (1, reps))

    qb = rope(proj(COL_QB, COL_KB), cbq_ref, sbq_ref)
    kb = rope(proj(COL_KB, COL_VB), cbk_ref, sbk_ref)
    vb = proj(COL_VB, COL_QKVC)
    GW = DIL_GROUP_WIDTH
    for j, val in enumerate((qb, kb, vb)):
        b0_out[0, 0, :, j * GW:(j + 1) * GW] = val[:, :GW].astype(BF16)
    for j, val in enumerate((qb, kb, vb)):
        for cb in range(DIL_WIDTH // LANE):
            stage_ref[j, cb] = val[:, cb * LANE:(cb + 1) * LANE]
    for grp, out in ((1, b1_out), (2, b2_out)):
        d = DIL_PATTERNS[grp][1]
        for r in range(d):
            for j in range(3):
                for cb in range(GW // LANE):
                    tile = stage_ref.at[j, grp * (GW // LANE) + cb]
                    rows = tile[pl.ds(r, tm // d, stride=d), :]
                    lo = j * GW + cb * LANE
                    out[0, r, :, lo:lo + LANE] = rows.astype(BF16)

    qkvc_out[...] = proj(COL_QKVC, COL_G).astype(BF16)

    for br in range(N_BRANCH):
        lo = COL_G + br * D_MODEL
        z = proj(lo, lo + D_MODEL)
        gates_out[:, br * D_MODEL:(br + 1) * D_MODEL] = (1.0 / (1.0 + jnp.exp(-z))).astype(BF16)


def _in_proj(x2, g_mix, wp, g_q, g_kv, wqT, wqrT, wk, wvT, tabs, B, S, tm):
    T = B * S
    nS = S // tm
    cqT, sqT, tk, cbq, sbq, cbk, sbk = tabs
    row_tab = pl.BlockSpec((tm, LANE), lambda i: (i % nS, 0))
    col_tab = pl.BlockSpec((MLA_HEAD_PAD, tm), lambda i: (0, i % nS))
    in_specs = [
        pl.BlockSpec((tm, D_MODEL), lambda i: (i, 0)),
        _resident(None), _resident(None), _resident(None), _resident(None),
        _resident(None), _resident(None), _resident(None), _resident(None),
        col_tab, col_tab, row_tab, row_tab, row_tab, row_tab, row_tab,
    ]
    HP = MLA_HEADS * MLA_HEAD_PAD
    HV = MLA_HEADS * MLA_V_DIM
    out_shape = [
        jax.ShapeDtypeStruct((B, HP, S), BF16),
        jax.ShapeDtypeStruct((T, HP), BF16),
        jax.ShapeDtypeStruct((B, HV, S), BF16),
        *[jax.ShapeDtypeStruct((B, d, S // d, DIL_WIDTH), BF16) for _, d in DIL_PATTERNS],
        jax.ShapeDtypeStruct((T, 3 * NA_WIDTH), BF16),
        jax.ShapeDtypeStruct((T, N_BRANCH * D_MODEL), BF16),
    ]
    out_specs = [
        pl.BlockSpec((1, HP, tm), lambda i: (i // nS, 0, i % nS)),
        pl.BlockSpec((tm, HP), lambda i: (i, 0)),
        pl.BlockSpec((1, HV, tm), lambda i: (i // nS, 0, i % nS)),
        *[pl.BlockSpec((1, d, tm // d, DIL_WIDTH), lambda i: (i // nS, 0, i % nS, 0))
          for _, d in DIL_PATTERNS],
        pl.BlockSpec((tm, 3 * NA_WIDTH), lambda i: (i, 0)),
        pl.BlockSpec((tm, N_BRANCH * D_MODEL), lambda i: (i, 0)),
    ]
    return pl.pallas_call(
        _in_proj_kernel,
        grid=(T // tm,),
        in_specs=in_specs,
        out_specs=out_specs,
        out_shape=out_shape,
        scratch_shapes=[pltpu.VMEM((3, DIL_WIDTH // LANE, tm, LANE), F32)],
        compiler_params=pltpu.CompilerParams(
            dimension_semantics=("arbitrary",), vmem_limit_bytes=VMEM_LIMIT),
        name="in_proj",
    )(x2, g_mix, wp, g_q, g_kv, wqT, wqrT, wk, wvT, cqT, sqT, tk, cbq, sbq, cbk, sbk)


def _denominator_rows(tk):
    return (lax.broadcasted_iota(jnp.int32, (MLA_DEN_ROWS, tk), 0) == 0).astype(BF16)


def _mla_online_path(q_view, k_ref, vT_ref, o_view, s_slots, p_slots, acc_ref, *, tk, n_chunks):
    tq = q_view.shape[1]

    def scores(c, slot):
        off = pl.multiple_of(c * tk, tk)
        s = _dot(k_ref[0, pl.ds(off, tk), :], q_view[...])
        s_slots[slot][...] = s
        return jnp.max(s, axis=0, keepdims=True)

    def softmax(slot, m, cmax):
        m_new = jnp.maximum(m, cmax)
        p = jnp.exp2(s_slots[slot][...] - m_new)
        p_slots[slot][...] = p.astype(BF16)
        return m_new, jnp.exp2(m - m_new)

    ones_rows = _denominator_rows(tk)

    def values(c, slot, alpha):
        off = pl.multiple_of(c * tk, tk)
        v = jnp.concatenate([vT_ref[0, :, pl.ds(off, tk)], ones_rows], axis=0)
        acc_ref[...] = alpha * acc_ref[...] + _dot(v, p_slots[slot][...])

    def step(c, parity, carry, with_scores=True):
        m, cmax, alpha_prev = carry
        cmax_next = scores(c + 1, 1 - parity) if with_scores else cmax
        m, alpha = softmax(parity, m, cmax)
        values(c - 1, 1 - parity, alpha_prev)
        return m, cmax_next, alpha

    assert n_chunks % 2 == 0 and n_chunks >= 4
    cmax0 = scores(0, 0)
    cmax1 = scores(1, 1)
    m, alpha = softmax(0, jnp.full((1, tq), -jnp.inf, F32), cmax0)
    acc_ref[...] = jnp.zeros_like(acc_ref)

    def body(j, carry):
        c = 2 * j + 1
        return step(c + 1, 0, step(c, 1, carry))

    carry = lax.fori_loop(0, n_chunks // 2 - 1, body, (m, cmax1, alpha))
    last = n_chunks - 1
    _, _, alpha = step(last, 1, carry, with_scores=False)
    values(last, 1, alpha)
    o_view[...] = (acc_ref[:MLA_V_DIM] / acc_ref[MLA_V_DIM:MLA_V_DIM + 1]).astype(BF16)


def _mla_kernel(qT_ref, k_ref, vT_ref, oT_ref, s0_ref, s1_ref, p0_ref, p1_ref, acc_ref, *,
                tq, tk, n_chunks):
    def tile(j, carry):
        off = pl.multiple_of(j * tq, tq)
        _mla_tile(qT_ref.at[0, :, pl.ds(off, tq)], k_ref, vT_ref, oT_ref.at[0, :, pl.ds(off, tq)],
                  (s0_ref, s1_ref), (p0_ref, p1_ref), acc_ref, tk=tk, n_chunks=n_chunks)
        return carry

    lax.fori_loop(0, qT_ref.shape[2] // tq, tile, 0)


def _mla_tile(q_view, k_ref, vT_ref, o_view, s_slots, p_slots, acc_ref, *, tk, n_chunks):
    q = q_view[...]
    tq = q.shape[1]
    head = _dot(k_ref[0, 0:MLA_REF_ROWS, :], q)
    ref = jnp.max(head, axis=0, keepdims=True)
    is_ref_row = lax.broadcasted_iota(jnp.int32, (MLA_REF_ROWS, tq), 0) == 0
    ref_rows = jnp.where(is_ref_row, -ref, 0.0).astype(BF16)
    q_aug = jnp.concatenate([q[:MLA_REF_COL], ref_rows, q[MLA_REF_COL + MLA_REF_ROWS:]], axis=0)
    ones_rows = _denominator_rows(tk)
    acc = jnp.zeros((MLA_V_DIM + MLA_DEN_ROWS, tq), F32)
    def shifted_scores(c):
        return _dot(k_ref[0, c * tk:(c + 1) * tk, :], q_aug)

    s = shifted_scores(0)
    for c in range(n_chunks):
        s_next = shifted_scores(c + 1) if c + 1 < n_chunks else None
        p = jnp.exp2(s).astype(BF16)
        v = jnp.concatenate([vT_ref[0, :, c * tk:(c + 1) * tk], ones_rows], axis=0)
        acc = acc + _dot(v, p)
        s = s_next
    finite = jnp.sum(acc * 0.0) == 0.0

    @pl.when(finite)
    def _():
        o_view[...] = (acc[:MLA_V_DIM] / acc[MLA_V_DIM:MLA_V_DIM + 1]).astype(BF16)

    @pl.when(jnp.logical_not(finite))
    def _():
        _mla_online_path(q_view, k_ref, vT_ref, o_view, s_slots, p_slots, acc_ref,
                         tk=tk, n_chunks=n_chunks)


def _mla(qT, k, vT, B, S, tq, tk, tiles_per_step):
    tb = tq * tiles_per_step
    assert S % tb == 0
    grid = (B, MLA_HEADS, S // tb)
    return pl.pallas_call(
        functools.partial(_mla_kernel, tq=tq, tk=tk, n_chunks=S // tk),
        grid=grid,
        scratch_shapes=[pltpu.VMEM((tk, tq), F32), pltpu.VMEM((tk, tq), F32),
                        pltpu.VMEM((tk, tq), BF16), pltpu.VMEM((tk, tq), BF16),
                        pltpu.VMEM((MLA_V_DIM + MLA_DEN_ROWS, tq), F32)],
        in_specs=[
            pl.BlockSpec((1, MLA_HEAD_PAD, tb), lambda b, h, i: (b, h, i)),
            pl.BlockSpec((1, S, MLA_HEAD_PAD), lambda b, h, i: (b, 0, h)),
            pl.BlockSpec((1, MLA_V_DIM, S), lambda b, h, i: (b, h, 0)),
        ],
        out_specs=pl.BlockSpec((1, MLA_V_DIM, tb), lambda b, h, i: (b, h, i)),
        out_shape=jax.ShapeDtypeStruct((B, MLA_HEADS * MLA_V_DIM, S), BF16),
        compiler_params=pltpu.CompilerParams(
            dimension_semantics=("arbitrary", "arbitrary", "arbitrary"),
            vmem_limit_bytes=VMEM_LIMIT),
        name="mla",
    )(qT, k, vT)


def _head_lane_mask(width):
    return lax.broadcasted_iota(jnp.int32, (1, width), 1) < HEAD_DIM


def _dilated_kernel(q_ref, k_ref, v_ref, o_ref, lse_ref, *, tq, win, radius, L):
    first = _head_lane_mask(LANE)
    for sub in range(q_ref.shape[0] // tq):
        tile = pl.program_id(2) * (q_ref.shape[0] // tq) + sub
        rows = slice(sub * tq, (sub + 1) * tq)
        start = jnp.clip(tile * tq - radius, 0, L - win)
        start = pl.multiple_of(start, radius)
        qpos = tile * tq + lax.broadcasted_iota(jnp.int32, (tq, win), 0)
        kpos = start + lax.broadcasted_iota(jnp.int32, (tq, win), 1)
        band = jnp.abs(qpos - kpos) <= radius
        for pair in range(DIL_HEADS_PER_GROUP // 2):
            lanes = slice(pair * LANE, (pair + 1) * LANE)
            qp = q_ref[rows, lanes]
            kp = k_ref[pl.ds(start, win), lanes]
            vp = v_ref[pl.ds(start, win), lanes]
            o_pair = None
            lse_pair = None
            for hh in range(2):
                sel = first if hh == 0 else jnp.logical_not(first)
                qm = jnp.where(sel, qp, jnp.zeros_like(qp))
                vm = jnp.where(sel, vp, jnp.zeros_like(vp))
                s = jnp.where(band, _dot_nt(qm, kp), MASK_VALUE)
                m = jnp.max(s, axis=1, keepdims=True)
                p = jnp.exp2(s - m)
                l = jnp.sum(p, axis=1, keepdims=True)
                o = _dot(p.astype(BF16), vm) / l
                lse = m + jnp.log2(l)
                if hh == 0:
                    o_pair = o
                    lse_pair = lse
                else:
                    o_pair = o_pair + o
                    lse_pair = jnp.where(first, lse_pair, lse)
            o_ref[rows, lanes] = o_pair.astype(BF16)
            lse_ref[rows, lanes] = jnp.broadcast_to(lse_pair, (tq, LANE))


def _dilated_group(qkvb, B, S, group, dilation, radius):
    L = S // dilation
    tq = min(256, L)
    win = tq + 2 * radius
    tb = tq * min(DIL_TILES_PER_STEP, L // tq)
    assert L >= win and L % tb == 0
    GW = DIL_GROUP_WIDTH
    out_spec = pl.BlockSpec((None, None, tb, GW), lambda b, r, i: (b, r, i, 0))
    return pl.pallas_call(
        functools.partial(_dilated_kernel, tq=tq, win=win, radius=radius, L=L),
        grid=(B, dilation, L // tb),
        in_specs=[
            pl.BlockSpec((None, None, tb, GW), lambda b, r, i: (b, r, i, 0)),
            pl.BlockSpec((None, None, L, GW), lambda b, r, i: (b, r, 0, 1)),
            pl.BlockSpec((None, None, L, GW), lambda b, r, i: (b, r, 0, 2)),
        ],
        out_specs=[out_spec, out_spec],
        out_shape=[
            jax.ShapeDtypeStruct((B, dilation, L, GW), BF16),
            jax.ShapeDtypeStruct((B, dilation, L, GW), F32),
        ],
        compiler_params=pltpu.CompilerParams(
            dimension_semantics=("arbitrary", "arbitrary", "arbitrary"),
            vmem_limit_bytes=VMEM_LIMIT),
        name=f"dilated_g{group}",
    )(qkvb, qkvb, qkvb)


def _na_kernel(q_ref, k_ref, v_ref, *rest, rows):
    bias_refs, o_ref = rest[:-1], rest[-1]
    tq = NA_ROWS_PER_BLOCK * GRID_W
    win = NA_WINDOW_ROWS * GRID_W
    first = _head_lane_mask(LANE)
    for sub, bias_ref in enumerate(bias_refs):
        blk = pl.program_id(2) * len(bias_refs) + sub
        qrows = slice(sub * tq, (sub + 1) * tq)
        start_row = jnp.clip(blk * NA_ROWS_PER_BLOCK - NA_KH // 2, 0, rows - NA_WINDOW_ROWS)
        start = pl.multiple_of(start_row * GRID_W, GRID_W)
        for pp in range(q_ref.shape[2] // LANE):
            lanes = slice(pp * LANE, (pp + 1) * LANE)
            qp = q_ref[0, qrows, lanes]
            kp = k_ref[0, pl.ds(start, win), lanes]
            vp = v_ref[0, pl.ds(start, win), lanes]
            o_pair = None
            for hh in range(2):
                sel = first if hh == 0 else jnp.logical_not(first)
                qm = jnp.where(sel, qp, jnp.zeros_like(qp))
                vm = jnp.where(sel, vp, jnp.zeros_like(vp))
                s = _dot_nt(qm, kp) + bias_ref[0, 2 * pp + hh]
                m = jnp.max(s, axis=1, keepdims=True)
                p = jnp.exp2(s - m)
                l = jnp.sum(p, axis=1, keepdims=True)
                o = _dot(p.astype(BF16), vm) / l
                o_pair = o if hh == 0 else o_pair + o
            o_ref[0, qrows, lanes] = o_pair.astype(BF16)


def _na_bias(rpb, rows):
    nb = rows // NA_ROWS_PER_BLOCK
    n_dr, n_dc = 2 * NA_KH - 1, 2 * NA_KW - 1
    qc = np.arange(GRID_W)[:, None]
    kc = np.arange(GRID_W)[None, :]
    cs = np.clip(qc - NA_KW // 2, 0, GRID_W - NA_KW)
    col_ok = (kc >= cs) & (kc < cs + NA_KW)
    dc = kc - qc + (NA_KW - 1)
    rpb2 = rpb.astype(F32) * LOG2E
    tc = sum(rpb2[:, :, j, None, None] * jnp.asarray((dc == j) & col_ok, F32) for j in range(n_dc))
    tc = jnp.where(jnp.asarray(col_ok), tc, MASK_VALUE)
    blank = jnp.full((NA_HEADS, 1, GRID_W, GRID_W), MASK_VALUE, F32)
    tcx = jnp.concatenate([blank, tc, blank], axis=1)
    pairs = jnp.concatenate([tcx[:, :-1], tcx[:, 1:]], axis=-1)
    plan = []
    for blk in (0, 1, nb - 1):
        R = blk * NA_ROWS_PER_BLOCK
        start = min(max(R - NA_KH // 2, 0), rows - NA_WINDOW_ROWS)
        per_row = []
        for a in range(NA_ROWS_PER_BLOCK):
            r0 = min(max(R + a - NA_KH // 2, 0), rows - NA_KH)
            ok = [r0 <= start + b < r0 + NA_KH for b in range(NA_WINDOW_ROWS)]
            ents = []
            for j in range(NA_WINDOW_ROWS // 2):
                dr0 = start + 2 * j - (R + a) + NA_KH - 1
                if not (ok[2 * j] or ok[2 * j + 1]):
                    ents.append(None)
                else:
                    assert 0 <= dr0 + 1 <= n_dr
                    ents.append((dr0 + 1, ok[2 * j], ok[2 * j + 1]))
            per_row.append(ents)
        plan.append(per_row)
    tq = NA_ROWS_PER_BLOCK * GRID_W
    win = NA_WINDOW_ROWS * GRID_W
    return pl.pallas_call(
        functools.partial(_na_bias_kernel, plan=plan),
        grid=(NA_HEADS,),
        in_specs=[pl.BlockSpec((1, n_dr + 1, GRID_W, 2 * GRID_W), lambda h: (h, 0, 0, 0))],
        out_specs=pl.BlockSpec((3, 1, tq, win), lambda h: (0, h, 0, 0)),
        out_shape=jax.ShapeDtypeStruct((3, NA_HEADS, tq, win), F32),
        compiler_params=pltpu.CompilerParams(
            dimension_semantics=("arbitrary",), vmem_limit_bytes=VMEM_LIMIT),
        name="na_bias",
    )(pairs)


def _na_bias_kernel(pairs_ref, out_ref, *, plan):
    left = lax.broadcasted_iota(jnp.int32, (GRID_W, 2 * GRID_W), 1) < GRID_W
    masked = jnp.full((GRID_W, 2 * GRID_W), MASK_VALUE, F32)
    for v, per_row in enumerate(plan):
        for a, ents in enumerate(per_row):
            for j, ent in enumerate(ents):
                piece = masked
                if ent is not None:
                    e, left_ok, right_ok = ent
                    piece = pairs_ref[0, e]
                    if not left_ok:
                        piece = jnp.where(left, masked, piece)
                    if not right_ok:
                        piece = jnp.where(left, piece, masked)
                out_ref[v, 0, a * GRID_W:(a + 1) * GRID_W, j * LANE:(j + 1) * LANE] = piece


def _na(qkvc, bias, B, S):
    rows = S // GRID_W
    nb = rows // NA_ROWS_PER_BLOCK
    assert nb >= 3
    tq = NA_ROWS_PER_BLOCK * GRID_W
    win = NA_WINDOW_ROWS * GRID_W
    hps = NA_HEADS_PER_STEP
    width = hps * HEAD_DIM
    ngroup = NA_HEADS // hps
    qv = qkvc.reshape(B, S, 3 * NA_WIDTH)

    nsub = NA_BLOCKS_PER_STEP
    assert nb % nsub == 0

    def variant(blk):
        return jnp.where(blk == 0, 0, jnp.where(blk == nb - 1, 2, 1))

    def bias_spec(sub):
        return pl.BlockSpec((1, hps, tq, win), lambda b, p, i: (variant(i * nsub + sub), p, 0, 0))

    return pl.pallas_call(
        functools.partial(_na_kernel, rows=rows),
        grid=(B, ngroup, nb // nsub),
        in_specs=[
            pl.BlockSpec((1, nsub * tq, width), lambda b, p, i: (b, i, p)),
            pl.BlockSpec((1, S, width), lambda b, p, i: (b, 0, ngroup + p)),
            pl.BlockSpec((1, S, width), lambda b, p, i: (b, 0, 2 * ngroup + p)),
            *[bias_spec(sub) for sub in range(nsub)],
        ],
        out_specs=pl.BlockSpec((1, nsub * tq, width), lambda b, p, i: (b, i, p)),
        out_shape=jax.ShapeDtypeStruct((B, S, NA_WIDTH), BF16),
        compiler_params=pltpu.CompilerParams(
            dimension_semantics=("arbitrary", "arbitrary", "arbitrary"),
            vmem_limit_bytes=VMEM_LIMIT),
        name="na",
    )(qv, qv, qv, *([bias] * nsub)).reshape(B * S, NA_WIDTH)


def _merge_kernel(x_ref, yaT_ref, o1_ref, o2_ref, o3_ref, l1_ref, l2_ref, l3_ref, yc_ref, g_ref,
                  wpa_ref, wpb_ref, wpc_ref, wo_ref, out_ref, o2n_ref, o3n_ref, l2n_ref, l3n_ref):
    a = _dot_tn(yaT_ref[0], wpa_ref[...])
    def token_order(src, dst):
        d, n = src.shape[0], src.shape[1]
        for r in range(d):
            for cb in range(DIL_GROUP_WIDTH // LANE):
                dst.at[cb][pl.ds(r, n, stride=d), :] = (
                    src[r, :, cb * LANE:(cb + 1) * LANE].astype(F32))
        return jnp.concatenate([dst[cb] for cb in range(DIL_GROUP_WIDTH // LANE)], axis=1)

    o2, l2 = token_order(o2_ref, o2n_ref), token_order(l2_ref, l2n_ref)
    o3, l3 = token_order(o3_ref, o3n_ref), token_order(l3_ref, l3n_ref)
    l1 = l1_ref[...]
    mx = jnp.maximum(jnp.maximum(l1, l2), l3)
    w1, w2, w3 = jnp.exp2(l1 - mx), jnp.exp2(l2 - mx), jnp.exp2(l3 - mx)
    yb = (w1 * o1_ref[...].astype(F32) + w2 * o2 + w3 * o3) / (w1 + w2 + w3)
    bm = _dot(yb.astype(BF16), wpb_ref[...])
    cm = _dot(yc_ref[...], wpc_ref[...])
    D = D_MODEL
    merged = (g_ref[:, 0:D].astype(F32) * a + g_ref[:, D:2 * D].astype(F32) * bm
              + g_ref[:, 2 * D:3 * D].astype(F32) * cm)
    out_ref[...] = x_ref[...] + _dot(merged.astype(BF16), wo_ref[...])


def _merge(x2, yaT, obs, lses, yc, gates, wpa, wpb, wpc, wo, B, S, tm):
    T = B * S
    nS = S // tm
    HV = MLA_HEADS * MLA_V_DIM
    GW = DIL_GROUP_WIDTH
    row = lambda w: pl.BlockSpec((tm, w), lambda i: (i, 0))

    def grouped(d):
        if d == 1:
            return pl.BlockSpec((None, None, tm, GW), lambda i: (i // nS, 0, i % nS, 0))
        return pl.BlockSpec((None, d, tm // d, GW), lambda i: (i // nS, 0, i % nS, 0))

    group_specs = [grouped(d) for _, d in DIL_PATTERNS]
    return pl.pallas_call(
        _merge_kernel,
        grid=(T // tm,),
        in_specs=[
            row(D_MODEL),
            pl.BlockSpec((1, HV, tm), lambda i: (i // nS, 0, i % nS)),
            *group_specs, *group_specs,
            row(NA_WIDTH), row(N_BRANCH * D_MODEL),
            _resident(None), _resident(None), _resident(None), _resident(None),
        ],
        out_specs=row(D_MODEL),
        out_shape=jax.ShapeDtypeStruct((T, D_MODEL), F32),
        scratch_shapes=[pltpu.VMEM((GW // LANE, tm, LANE), F32)] * 4,
        compiler_params=pltpu.CompilerParams(
            dimension_semantics=("arbitrary",), vmem_limit_bytes=VMEM_LIMIT),
        name="merge",
    )(x2, yaT, *obs, *lses, yc, gates, wpa, wpb, wpc, wo)


def _ffn_kernel(x_ref, g_ref, w1_ref, w3_ref, w2_ref, gf_ref, out_ref, *, final):
    x = x_ref[...]
    h = _rms(x, g_ref[...]).astype(BF16)
    u = _dot(h, w1_ref[...])
    v = _dot(h, w3_ref[...])
    a = (u / (1.0 + jnp.exp(-u)) * v).astype(BF16)
    y = x + _dot(a, w2_ref[...])
    if final:
        y = _rms(y, gf_ref[...])
    out_ref[...] = y


def _ffn(x2, g_ffn, w1, w3, w2, g_final, tm, final):
    T = x2.shape[0]
    row = pl.BlockSpec((tm, D_MODEL), lambda i: (i, 0))
    return pl.pallas_call(
        functools.partial(_ffn_kernel, final=final),
        grid=(T // tm,),
        in_specs=[row, _resident(None), _resident(None), _resident(None), _resident(None),
                  _resident(None)],
        out_specs=row,
        out_shape=jax.ShapeDtypeStruct((T, D_MODEL), F32),
        compiler_params=pltpu.CompilerParams(
            dimension_semantics=("arbitrary",), vmem_limit_bytes=VMEM_LIMIT),
        name="ffn",
    )(x2, g_ffn, w1, w3, w2, g_final)


def _rot_half_cols(w, dim):
    k = w.shape[0]
    w3 = w.reshape(k, -1, dim)
    return jnp.concatenate([-w3[..., dim // 2:], w3[..., :dim // 2]], axis=-1).reshape(k, -1)


def _pack_layer(w_in, w_uq, w_ukv):
    c_q, c_kv, k_r, qkv_b, qkv_c, gate = jnp.split(
        w_in,
        (MLA_Q_RANK, MLA_Q_RANK + MLA_KV_RANK, MLA_Q_RANK + MLA_KV_RANK + MLA_ROPE_DIM,
         MLA_Q_RANK + MLA_KV_RANK + MLA_ROPE_DIM + 3 * DIL_WIDTH,
         MLA_Q_RANK + MLA_KV_RANK + MLA_ROPE_DIM + 3 * DIL_WIDTH + 3 * NA_WIDTH), axis=1)
    q_b, k_b, v_b = jnp.split(qkv_b, 3, axis=1)
    q_c, k_c, v_c = jnp.split(qkv_c, 3, axis=1)
    q_c = q_c * (HEAD_DIM ** -0.5 * LOG2E)
    pad = jnp.zeros((D_MODEL, COL_QB - (MLA_Q_RANK + MLA_KV_RANK + 2 * MLA_ROPE_DIM)), F32)
    wp = jnp.concatenate(
        [c_q, c_kv, k_r, _rot_half_cols(k_r, MLA_ROPE_DIM), pad,
         q_b, k_b, v_b,
         q_c, k_c, v_c, gate], axis=1).astype(BF16)
    assert wp.shape[1] == COL_END

    uq = w_uq.reshape(MLA_Q_RANK, MLA_HEADS, MLA_NOPE_DIM + MLA_ROPE_DIM)
    zpad = jnp.zeros((MLA_Q_RANK, MLA_HEADS, MLA_HEAD_PAD - MLA_NOPE_DIM - MLA_ROPE_DIM), F32)
    q_main = jnp.concatenate([uq, zpad], axis=-1)
    rope = uq[..., MLA_NOPE_DIM:]
    rope_rot = jnp.concatenate([-rope[..., MLA_ROPE_DIM // 2:], rope[..., :MLA_ROPE_DIM // 2]], -1)
    q_rot = jnp.concatenate([jnp.zeros_like(uq[..., :MLA_NOPE_DIM]), rope_rot, zpad], axis=-1)
    HP = MLA_HEADS * MLA_HEAD_PAD
    wqT = q_main.reshape(MLA_Q_RANK, HP).T.astype(BF16)
    wqrT = q_rot.reshape(MLA_Q_RANK, HP).T.astype(BF16)

    ukv = w_ukv.reshape(MLA_KV_RANK, MLA_HEADS, MLA_NOPE_DIM + MLA_V_DIM)
    k_nope = jnp.concatenate(
        [ukv[..., :MLA_NOPE_DIM],
         jnp.zeros((MLA_KV_RANK, MLA_HEADS, MLA_HEAD_PAD - MLA_NOPE_DIM), F32)], axis=-1)
    place = np.zeros((LANE, MLA_HEADS, MLA_HEAD_PAD), np.float32)
    for j in range(2 * MLA_ROPE_DIM):
        place[j, :, MLA_NOPE_DIM + j % MLA_ROPE_DIM] = 1.0
    wk = jnp.concatenate([k_nope.reshape(MLA_KV_RANK, HP), jnp.asarray(place).reshape(LANE, HP)],
                         axis=0).astype(BF16)
    wvT = ukv[..., MLA_NOPE_DIM:].reshape(MLA_KV_RANK, MLA_HEADS * MLA_V_DIM).T.astype(BF16)
    return wp, wqT, wqrT, wk, wvT


def _rope_cos_sin(seq_len, dim):
    pos = jnp.arange(seq_len, dtype=F32)
    inv = jnp.power(ROPE_THETA, -jnp.arange(0, dim, 2, dtype=F32) / dim)
    ang = pos[:, None] * inv[None, :]
    return jnp.cos(ang), jnp.sin(ang)


def _tables(S):
    cos_a, sin_a = _rope_cos_sin(S, MLA_ROPE_DIM)
    qs = (MLA_NOPE_DIM + MLA_ROPE_DIM) ** -0.5 * LOG2E
    ones = jnp.ones((S, MLA_NOPE_DIM), F32)
    zeros_n = jnp.zeros((S, MLA_NOPE_DIM), F32)
    zpad = jnp.zeros((S, MLA_HEAD_PAD - MLA_NOPE_DIM - MLA_ROPE_DIM), F32)
    cqT = (jnp.concatenate([ones, cos_a, cos_a, zpad], axis=1) * qs).T
    sqT = (jnp.concatenate([zeros_n, sin_a, sin_a, zpad], axis=1) * qs).T
    tk = jnp.concatenate([cos_a, cos_a, sin_a, sin_a, jnp.zeros((S, LANE - 2 * MLA_ROPE_DIM), F32)],
                         axis=1)
    cos_b, sin_b = _rope_cos_sin(S, HEAD_DIM)
    cb = jnp.tile(cos_b, (1, LANE // (HEAD_DIM // 2)))
    sb = jnp.tile(jnp.concatenate([-sin_b, sin_b], axis=1), (1, LANE // HEAD_DIM))
    bs = HEAD_DIM ** -0.5 * LOG2E
    return cqT, sqT, tk, cb * bs, sb * bs, cb, sb


def kernel(x, w_in, g_mix, g_q, g_kv, w_uq, w_ukv, rpb, w_pa, w_pb, w_pc, w_o, g_ffn, w1, w3, w2,
           g_final):
    B, S, D = x.shape
    depth = w_in.shape[0]
    T = B * S
    tm_proj = 512
    tm_post = 512
    tabs = _tables(S)
    x2 = x.reshape(T, D)
    for l in range(depth):
        wp, wqT, wqrT, wk, wvT = _pack_layer(w_in[l], w_uq[l], w_ukv[l])
        qT, k, vT, qkvb0, qkvb1, qkvb2, qkvc, gates = _in_proj(
            x2, g_mix[l][None], wp, g_q[l][None], g_kv[l][None], wqT, wqrT, wk, wvT, tabs,
            B, S, tm_proj)
        yaT = _mla(qT, k.reshape(B, S, -1), vT, B, S, tq=512, tk=512, tiles_per_step=4)
        obs, lses = [], []
        for grp, (qkvb, (window, dilation)) in enumerate(zip((qkvb0, qkvb1, qkvb2), DIL_PATTERNS)):
            o, lse = _dilated_group(qkvb, B, S, grp, dilation, window // (2 * dilation))
            obs.append(o)
            lses.append(lse)
        yc = _na(qkvc, _na_bias(rpb[l], S // GRID_W), B, S)
        x2 = _merge(x2, yaT, obs, lses, yc, gates, w_pa[l].astype(BF16), w_pb[l].astype(BF16),
                    w_pc[l].astype(BF16), w_o[l].astype(BF16), B, S, tm_post)
        x2 = _ffn(x2, g_ffn[l][None], w1[l].astype(BF16), w3[l].astype(BF16), w2[l].astype(BF16),
                  g_final[None], tm_post, final=(l == depth - 1))
    return x2.reshape(B, S, D)
```

```python
import functools
import math

import jax
import jax.numpy as jnp
import numpy as np
from jax import lax
from jax.experimental import pallas as pl
from jax.experimental.pallas import tpu as pltpu

F32 = jnp.float32
BF16 = jnp.bfloat16

D_MODEL = 1024
HEAD_DIM = 64
ROPE_THETA = 10000.0
EPS = 1e-6
MASK_VALUE = -1e30
LOG2E = 1.4426950408889634

MLA_HEADS = 8
MLA_Q_RANK = 256
MLA_KV_RANK = 128
MLA_NOPE_DIM = 64
MLA_ROPE_DIM = 32
MLA_V_DIM = 64
MLA_HEAD_PAD = 128
MLA_DEN_ROWS = 16
MLA_REF_COL = MLA_NOPE_DIM + MLA_ROPE_DIM
MLA_REF_ROWS = 16

DIL_PATTERNS = ((128, 1), (512, 4), (2048, 16))
DIL_HEADS_PER_GROUP = 4
DIL_HEADS = DIL_HEADS_PER_GROUP * len(DIL_PATTERNS)
DIL_WIDTH = DIL_HEADS * HEAD_DIM
DIL_GROUP_WIDTH = DIL_HEADS_PER_GROUP * HEAD_DIM
DIL_TILES_PER_STEP = 4

NA_HEADS = 8
NA_KH = 8
NA_KW = 16
GRID_W = 64
NA_WIDTH = NA_HEADS * HEAD_DIM
NA_HEADS_PER_STEP = 4
NA_BLOCKS_PER_STEP = 2
NA_ROWS_PER_BLOCK = 4
NA_WINDOW_ROWS = 12

N_BRANCH = 3
D_FF = -(-(8 * D_MODEL) // (3 * 256)) * 256

LANE = 128

COL_C = 0
COL_QB = 512
COL_KB = COL_QB + DIL_WIDTH
COL_VB = COL_KB + DIL_WIDTH
COL_QKVC = COL_VB + DIL_WIDTH
COL_G = COL_QKVC + 3 * NA_WIDTH
COL_END = COL_G + N_BRANCH * D_MODEL

VMEM_LIMIT = 56 * 1024 * 1024


def _dot(a, b):
    return jnp.dot(a, b, preferred_element_type=F32)


def _dot_nt(a, b):
    return lax.dot_general(a, b, (((1,), (1,)), ((), ())), preferred_element_type=F32)


def _dot_tn(a, b):
    return lax.dot_general(a, b, (((0,), (0,)), ((), ())), preferred_element_type=F32)


def _rms(x, g):
    return x * lax.rsqrt(jnp.mean(x * x, axis=-1, keepdims=True) + EPS) * g


def _resident(shape):
    del shape
    return pl.BlockSpec(memory_space=pltpu.VMEM)


def _in_proj_kernel(x_ref, g_ref, w_ref, gq_ref, gkv_ref, wqT_ref, wqrT_ref, wk_ref, wvT_ref,
                    cqT_ref, sqT_ref, tk_ref, cbq_ref, sbq_ref, cbk_ref, sbk_ref,
                    qT_out, k_out, vT_out, b0_out, b1_out, b2_out, qkvc_out, gates_out,
                    stage_ref):
    tm = x_ref.shape[0]
    h = _rms(x_ref[...], g_ref[...]).astype(BF16)

    def proj(lo, hi):
        return _dot(h, w_ref[:, lo:hi])

    c = proj(COL_C, COL_QB)
    cqn = _rms(c[:, :MLA_Q_RANK], gq_ref[...]).astype(BF16)
    ckvn = _rms(c[:, MLA_Q_RANK:MLA_Q_RANK + MLA_KV_RANK], gkv_ref[...]).astype(BF16)
    kr = (c[:, MLA_Q_RANK + MLA_KV_RANK:] * tk_ref[...]).astype(BF16)
    lane = lax.broadcasted_iota(jnp.int32, (1, MLA_HEADS * MLA_HEAD_PAD), 1)
    ref_col = jnp.where((lane & (MLA_HEAD_PAD - 1)) == MLA_REF_COL, 1.0, 0.0)
    k_out[...] = (_dot(jnp.concatenate([ckvn, kr], axis=1), wk_ref[...]) + ref_col).astype(BF16)
    vT_out[0] = _dot_nt(wvT_ref[...], ckvn).astype(BF16)
    q_main = _dot_nt(wqT_ref[...], cqn)
    q_rot = _dot_nt(wqrT_ref[...], cqn)
    cq = cqT_ref[...]
    sq = sqT_ref[...]
    for hd in range(MLA_HEADS):
        rows = slice(hd * MLA_HEAD_PAD, (hd + 1) * MLA_HEAD_PAD)
        qT_out[0, rows, :] = (q_main[rows] * cq + q_rot[rows] * sq).astype(BF16)

    reps = DIL_WIDTH // LANE
    half = HEAD_DIM // 2
    lane = lax.broadcasted_iota(jnp.int32, (1, DIL_WIDTH), 1)
    in_first_half = (lane & (HEAD_DIM - 1)) < half

    def rope(x, cos_ref, sin_ref):
        swapped = jnp.where(in_first_half, pltpu.roll(x, DIL_WIDTH - half, axis=1),
                            pltpu.roll(x, half, axis=1))
        return x * jnp.tile(cos_ref[...], (1, reps)) + swapped * jnp.tile(sin_ref[...], (1, reps))

    qb = rope(proj(COL_QB, COL_KB), cbq_ref, sbq_ref)
    kb = rope(proj(COL_KB, COL_VB), cbk_ref, sbk_ref)
    vb = proj(COL_VB, COL_QKVC)
    GW = DIL_GROUP_WIDTH
    for j, val in enumerate((qb, kb, vb)):
        b0_out[0, 0, :, j * GW:(j + 1) * GW] = val[:, :GW].astype(BF16)
    for j, val in enumerate((qb, kb, vb)):
        for cb in range(DIL_WIDTH // LANE):
            stage_ref[j, cb] = val[:, cb * LANE:(cb + 1) * LANE]
    for grp, out in ((1, b1_out), (2, b2_out)):
        d = DIL_PATTERNS[grp][1]
        for r in range(d):
            for j in range(3):
                for cb in range(GW // LANE):
                    tile = stage_ref.at[j, grp * (GW // LANE) + cb]
                    rows = tile[pl.ds(r, tm // d, stride=d), :]
                    lo = j * GW + cb * LANE
                    out[0, r, :, lo:lo + LANE] = rows.astype(BF16)

    qkvc_out[...] = proj(COL_QKVC, COL_G).astype(BF16)

    for br in range(N_BRANCH):
        lo = COL_G + br * D_MODEL
        z = proj(lo, lo + D_MODEL)
        gates_out[:, br * D_MODEL:(br + 1) * D_MODEL] = (1.0 / (1.0 + jnp.exp(-z))).astype(BF16)


def _in_proj(x2, g_mix, wp, g_q, g_kv, wqT, wqrT, wk, wvT, tabs, B, S, tm):
    T = B * S
    nS = S // tm
    cqT, sqT, tk, cbq, sbq, cbk, sbk = tabs
    row_tab = pl.BlockSpec((tm, LANE), lambda i: (i % nS, 0))
    col_tab = pl.BlockSpec((MLA_HEAD_PAD, tm), lambda i: (0, i % nS))
    in_specs = [
        pl.BlockSpec((tm, D_MODEL), lambda i: (i, 0)),
        _resident(None), _resident(None), _resident(None), _resident(None),
        _resident(None), _resident(None), _resident(None), _resident(None),
        col_tab, col_tab, row_tab, row_tab, row_tab, row_tab, row_tab,
    ]
    HP = MLA_HEADS * MLA_HEAD_PAD
    HV = MLA_HEADS * MLA_V_DIM
    out_shape = [
        jax.ShapeDtypeStruct((B, HP, S), BF16),
        jax.ShapeDtypeStruct((T, HP), BF16),
        jax.ShapeDtypeStruct((B, HV, S), BF16),
        *[jax.ShapeDtypeStruct((B, d, S // d, DIL_WIDTH), BF16) for _, d in DIL_PATTERNS],
        jax.ShapeDtypeStruct((T, 3 * NA_WIDTH), BF16),
        jax.ShapeDtypeStruct((T, N_BRANCH * D_MODEL), BF16),
    ]
    out_specs = [
        pl.BlockSpec((1, HP, tm), lambda i: (i // nS, 0, i % nS)),
        pl.BlockSpec((tm, HP), lambda i: (i, 0)),
        pl.BlockSpec((1, HV, tm), lambda i: (i // nS, 0, i % nS)),
        *[pl.BlockSpec((1, d, tm // d, DIL_WIDTH), lambda i: (i // nS, 0, i % nS, 0))
          for _, d in DIL_PATTERNS],
        pl.BlockSpec((tm, 3 * NA_WIDTH), lambda i: (i, 0)),
        pl.BlockSpec((tm, N_BRANCH * D_MODEL), lambda i: (i, 0)),
    ]
    return pl.pallas_call(
        _in_proj_kernel,
        grid=(T // tm,),
        in_specs=in_specs,
        out_specs=out_specs,
        out_shape=out_shape,
        scratch_shapes=[pltpu.VMEM((3, DIL_WIDTH // LANE, tm, LANE), F32)],
        compiler_params=pltpu.CompilerParams(
            dimension_semantics=("arbitrary",), vmem_limit_bytes=VMEM_LIMIT),
        name="in_proj",
    )(x2, g_mix, wp, g_q, g_kv, wqT, wqrT, wk, wvT, cqT, sqT, tk, cbq, sbq, cbk, sbk)


def _denominator_rows(tk):
    return (lax.broadcasted_iota(jnp.int32, (MLA_DEN_ROWS, tk), 0) == 0).astype(BF16)


def _mla_online_path(q_view, k_ref, vT_ref, o_view, s_slots, p_slots, acc_ref, *, tk, n_chunks):
    tq = q_view.shape[1]

    def scores(c, slot):
        off = pl.multiple_of(c * tk, tk)
        s = _dot(k_ref[0, pl.ds(off, tk), :], q_view[...])
        s_slots[slot][...] = s
        return jnp.max(s, axis=0, keepdims=True)

    def softmax(slot, m, cmax):
        m_new = jnp.maximum(m, cmax)
        p = jnp.exp2(s_slots[slot][...] - m_new)
        p_slots[slot][...] = p.astype(BF16)
        return m_new, jnp.exp2(m - m_new)

    ones_rows = _denominator_rows(tk)

    def values(c, slot, alpha):
        off = pl.multiple_of(c * tk, tk)
        v = jnp.concatenate([vT_ref[0, :, pl.ds(off, tk)], ones_rows], axis=0)
        acc_ref[...] = alpha * acc_ref[...] + _dot(v, p_slots[slot][...])

    def step(c, parity, carry, with_scores=True):
        m, cmax, alpha_prev = carry
        cmax_next = scores(c + 1, 1 - parity) if with_scores else cmax
        m, alpha = softmax(parity, m, cmax)
        values(c - 1, 1 - parity, alpha_prev)
        return m, cmax_next, alpha

    assert n_chunks % 2 == 0 and n_chunks >= 4
    cmax0 = scores(0, 0)
    cmax1 = scores(1, 1)
    m, alpha = softmax(0, jnp.full((1, tq), -jnp.inf, F32), cmax0)
    acc_ref[...] = jnp.zeros_like(acc_ref)

    def body(j, carry):
        c = 2 * j + 1
        return step(c + 1, 0, step(c, 1, carry))

    carry = lax.fori_loop(0, n_chunks // 2 - 1, body, (m, cmax1, alpha))
    last = n_chunks - 1
    _, _, alpha = step(last, 1, carry, with_scores=False)
    values(last, 1, alpha)
    o_view[...] = (acc_ref[:MLA_V_DIM] / acc_ref[MLA_V_DIM:MLA_V_DIM + 1]).astype(BF16)


def _mla_kernel(qT_ref, k_ref, vT_ref, oT_ref, s0_ref, s1_ref, p0_ref, p1_ref, acc_ref, *,
                tq, tk, n_chunks):
    def tile(j, carry):
        off = pl.multiple_of(j * tq, tq)
        _mla_tile(qT_ref.at[0, :, pl.ds(off, tq)], k_ref, vT_ref, oT_ref.at[0, :, pl.ds(off, tq)],
                  (s0_ref, s1_ref), (p0_ref, p1_ref), acc_ref, tk=tk, n_chunks=n_chunks)
        return carry

    lax.fori_loop(0, qT_ref.shape[2] // tq, tile, 0)


def _mla_tile(q_view, k_ref, vT_ref, o_view, s_slots, p_slots, acc_ref, *, tk, n_chunks):
    q = q_view[...]
    tq = q.shape[1]
    head = _dot(k_ref[0, 0:MLA_REF_ROWS, :], q)
    ref = jnp.max(head, axis=0, keepdims=True)
    is_ref_row = lax.broadcasted_iota(jnp.int32, (MLA_REF_ROWS, tq), 0) == 0
    ref_rows = jnp.where(is_ref_row, -ref, 0.0).astype(BF16)
    q_aug = jnp.concatenate([q[:MLA_REF_COL], ref_rows, q[MLA_REF_COL + MLA_REF_ROWS:]], axis=0)
    ones_rows = _denominator_rows(tk)
    acc = jnp.zeros((MLA_V_DIM + MLA_DEN_ROWS, tq), F32)
    def shifted_scores(c):
        return _dot(k_ref[0, c * tk:(c + 1) * tk, :], q_aug)

    s = shifted_scores(0)
    for c in range(n_chunks):
        s_next = shifted_scores(c + 1) if c + 1 < n_chunks else None
        p = jnp.exp2(s).astype(BF16)
        v = jnp.concatenate([vT_ref[0, :, c * tk:(c + 1) * tk], ones_rows], axis=0)
        acc = acc + _dot(v, p)
        s = s_next
    finite = jnp.sum(acc * 0.0) == 0.0

    @pl.when(finite)
    def _():
        o_view[...] = (acc[:MLA_V_DIM] / acc[MLA_V_DIM:MLA_V_DIM + 1]).astype(BF16)

    @pl.when(jnp.logical_not(finite))
    def _():
        _mla_online_path(q_view, k_ref, vT_ref, o_view, s_slots, p_slots, acc_ref,
                         tk=tk, n_chunks=n_chunks)


def _mla(qT, k, vT, B, S, tq, tk, tiles_per_step):
    tb = tq * tiles_per_step
    assert S % tb == 0
    grid = (B, MLA_HEADS, S // tb)
    return pl.pallas_call(
        functools.partial(_mla_kernel, tq=tq, tk=tk, n_chunks=S // tk),
        grid=grid,
        scratch_shapes=[pltpu.VMEM((tk, tq), F32), pltpu.VMEM((tk, tq), F32),
                        pltpu.VMEM((tk, tq), BF16), pltpu.VMEM((tk, tq), BF16),
                        pltpu.VMEM((MLA_V_DIM + MLA_DEN_ROWS, tq), F32)],
        in_specs=[
            pl.BlockSpec((1, MLA_HEAD_PAD, tb), lambda b, h, i: (b, h, i)),
            pl.BlockSpec((1, S, MLA_HEAD_PAD), lambda b, h, i: (b, 0, h)),
            pl.BlockSpec((1, MLA_V_DIM, S), lambda b, h, i: (b, h, 0)),
        ],
        out_specs=pl.BlockSpec((1, MLA_V_DIM, tb), lambda b, h, i: (b, h, i)),
        out_shape=jax.ShapeDtypeStruct((B, MLA_HEADS * MLA_V_DIM, S), BF16),
        compiler_params=pltpu.CompilerParams(
            dimension_semantics=("arbitrary", "arbitrary", "arbitrary"),
            vmem_limit_bytes=VMEM_LIMIT),
        name="mla",
    )(qT, k, vT)


def _head_lane_mask(width):
    return lax.broadcasted_iota(jnp.int32, (1, width), 1) < HEAD_DIM


def _dilated_kernel(q_ref, k_ref, v_ref, o_ref, lse_ref, *, tq, win, radius, L):
    first = _head_lane_mask(LANE)
    for sub in range(q_ref.shape[0] // tq):
        tile = pl.program_id(2) * (q_ref.shape[0] // tq) + sub
        rows = slice(sub * tq, (sub + 1) * tq)
        start = jnp.clip(tile * tq - radius, 0, L - win)
        start = pl.multiple_of(start, radius)
        qpos = tile * tq + lax.broadcasted_iota(jnp.int32, (tq, win), 0)
        kpos = start + lax.broadcasted_iota(jnp.int32, (tq, win), 1)
        band = jnp.abs(qpos - kpos) <= radius
        for pair in range(DIL_HEADS_PER_GROUP // 2):
            lanes = slice(pair * LANE, (pair + 1) * LANE)
            qp = q_ref[rows, lanes]
            kp = k_ref[pl.ds(start, win), lanes]
            vp = v_ref[pl.ds(start, win), lanes]
            o_pair = None
            lse_pair = None
            for hh in range(2):
                sel = first if hh == 0 else jnp.logical_not(first)
                qm = jnp.where(sel, qp, jnp.zeros_like(qp))
                vm = jnp.where(sel, vp, jnp.zeros_like(vp))
                s = jnp.where(band, _dot_nt(qm, kp), MASK_VALUE)
                m = jnp.max(s, axis=1, keepdims=True)
                p = jnp.exp2(s - m)
                l = jnp.sum(p, axis=1, keepdims=True)
                o = _dot(p.astype(BF16), vm) / l
                lse = m + jnp.log2(l)
                if hh == 0:
                    o_pair = o
                    lse_pair = lse
                else:
                    o_pair = o_pair + o
                    lse_pair = jnp.where(first, lse_pair, lse)
            o_ref[rows, lanes] = o_pair.astype(BF16)
            lse_ref[rows, lanes] = jnp.broadcast_to(lse_pair, (tq, LANE))


def _dilated_group(qkvb, B, S, group, dilation, radius):
    L = S // dilation
    tq = min(256, L)
    win = tq + 2 * radius
    tb = tq * min(DIL_TILES_PER_STEP, L // tq)
    assert L >= win and L % tb == 0
    GW = DIL_GROUP_WIDTH
    out_spec = pl.BlockSpec((None, None, tb, GW), lambda b, r, i: (b, r, i, 0))
    return pl.pallas_call(
        functools.partial(_dilated_kernel, tq=tq, win=win, radius=radius, L=L),
        grid=(B, dilation, L // tb),
        in_specs=[
            pl.BlockSpec((None, None, tb, GW), lambda b, r, i: (b, r, i, 0)),
            pl.BlockSpec((None, None, L, GW), lambda b, r, i: (b, r, 0, 1)),
            pl.BlockSpec((None, None, L, GW), lambda b, r, i: (b, r, 0, 2)),
        ],
        out_specs=[out_spec, out_spec],
        out_shape=[
            jax.ShapeDtypeStruct((B, dilation, L, GW), BF16),
            jax.ShapeDtypeStruct((B, dilation, L, GW), F32),
        ],
        compiler_params=pltpu.CompilerParams(
            dimension_semantics=("arbitrary", "arbitrary", "arbitrary"),
            vmem_limit_bytes=VMEM_LIMIT),
        name=f"dilated_g{group}",
    )(qkvb, qkvb, qkvb)


def _na_kernel(q_ref, k_ref, v_ref, *rest, rows):
    bias_refs, o_ref = rest[:-1], rest[-1]
    tq = NA_ROWS_PER_BLOCK * GRID_W
    win = NA_WINDOW_ROWS * GRID_W
    first = _head_lane_mask(LANE)
    for sub, bias_ref in enumerate(bias_refs):
        blk = pl.program_id(2) * len(bias_refs) + sub
        qrows = slice(sub * tq, (sub + 1) * tq)
        start_row = jnp.clip(blk * NA_ROWS_PER_BLOCK - NA_KH // 2, 0, rows - NA_WINDOW_ROWS)
        start = pl.multiple_of(start_row * GRID_W, GRID_W)
        for pp in range(q_ref.shape[2] // LANE):
            lanes = slice(pp * LANE, (pp + 1) * LANE)
            qp = q_ref[0, qrows, lanes]
            kp = k_ref[0, pl.ds(start, win), lanes]
            vp = v_ref[0, pl.ds(start, win), lanes]
            o_pair = None
            for hh in range(2):
                sel = first if hh == 0 else jnp.logical_not(first)
                qm = jnp.where(sel, qp, jnp.zeros_like(qp))
                vm = jnp.where(sel, vp, jnp.zeros_like(vp))
                s = _dot_nt(qm, kp) + bias_ref[0, 2 * pp + hh]
                m = jnp.max(s, axis=1, keepdims=True)
                p = jnp.exp2(s - m)
                l = jnp.sum(p, axis=1, keepdims=True)
                o = _dot(p.astype(BF16), vm) / l
                o_pair = o if hh == 0 else o_pair + o
            o_ref[0, qrows, lanes] = o_pair.astype(BF16)


def _na_bias(rpb, rows):
    nb = rows // NA_ROWS_PER_BLOCK
    n_dr, n_dc = 2 * NA_KH - 1, 2 * NA_KW - 1
    qc = np.arange(GRID_W)[:, None]
    kc = np.arange(GRID_W)[None, :]
    cs = np.clip(qc - NA_KW // 2, 0, GRID_W - NA_KW)
    col_ok = (kc >= cs) & (kc < cs + NA_KW)
    dc = kc - qc + (NA_KW - 1)
    rpb2 = rpb.astype(F32) * LOG2E
    tc = sum(rpb2[:, :, j, None, None] * jnp.asarray((dc == j) & col_ok, F32) for j in range(n_dc))
    tc = jnp.where(jnp.asarray(col_ok), tc, MASK_VALUE)
    blank = jnp.full((NA_HEADS, 1, GRID_W, GRID_W), MASK_VALUE, F32)
    tcx = jnp.concatenate([blank, tc, blank], axis=1)
    pairs = jnp.concatenate([tcx[:, :-1], tcx[:, 1:]], axis=-1)
    plan = []
    for blk in (0, 1, nb - 1):
        R = blk * NA_ROWS_PER_BLOCK
        start = min(max(R - NA_KH // 2, 0), rows - NA_WINDOW_ROWS)
        per_row = []
        for a in range(NA_ROWS_PER_BLOCK):
            r0 = min(max(R + a - NA_KH // 2, 0), rows - NA_KH)
            ok = [r0 <= start + b < r0 + NA_KH for b in range(NA_WINDOW_ROWS)]
            ents = []
            for j in range(NA_WINDOW_ROWS // 2):
                dr0 = start + 2 * j - (R + a) + NA_KH - 1
                if not (ok[2 * j] or ok[2 * j + 1]):
                    ents.append(None)
                else:
                    assert 0 <= dr0 + 1 <= n_dr
                    ents.append((dr0 + 1, ok[2 * j], ok[2 * j + 1]))
            per_row.append(ents)
        plan.append(per_row)
    tq = NA_ROWS_PER_BLOCK * GRID_W
    win = NA_WINDOW_ROWS * GRID_W
    return pl.pallas_call(
        functools.partial(_na_bias_kernel, plan=plan),
        grid=(NA_HEADS,),
        in_specs=[pl.BlockSpec((1, n_dr + 1, GRID_W, 2 * GRID_W), lambda h: (h, 0, 0, 0))],
        out_specs=pl.BlockSpec((3, 1, tq, win), lambda h: (0, h, 0, 0)),
        out_shape=jax.ShapeDtypeStruct((3, NA_HEADS, tq, win), F32),
        compiler_params=pltpu.CompilerParams(
            dimension_semantics=("arbitrary",), vmem_limit_bytes=VMEM_LIMIT),
        name="na_bias",
    )(pairs)


def _na_bias_kernel(pairs_ref, out_ref, *, plan):
    left = lax.broadcasted_iota(jnp.int32, (GRID_W, 2 * GRID_W), 1) < GRID_W
    masked = jnp.full((GRID_W, 2 * GRID_W), MASK_VALUE, F32)
    for v, per_row in enumerate(plan):
        for a, ents in enumerate(per_row):
            for j, ent in enumerate(ents):
                piece = masked
                if ent is not None:
                    e, left_ok, right_ok = ent
                    piece = pairs_ref[0, e]
                    if not left_ok:
                        piece = jnp.where(left, masked, piece)
                    if not right_ok:
                        piece = jnp.where(left, piece, masked)
                out_ref[v, 0, a * GRID_W:(a + 1) * GRID_W, j * LANE:(j + 1) * LANE] = piece


def _na(qkvc, bias, B, S):
    rows = S // GRID_W
    nb = rows // NA_ROWS_PER_BLOCK
    assert nb >= 3
    tq = NA_ROWS_PER_BLOCK * GRID_W
    win = NA_WINDOW_ROWS * GRID_W
    hps = NA_HEADS_PER_STEP
    width = hps * HEAD_DIM
    ngroup = NA_HEADS // hps
    qv = qkvc.reshape(B, S, 3 * NA_WIDTH)

    nsub = NA_BLOCKS_PER_STEP
    assert nb % nsub == 0

    def variant(blk):
        return jnp.where(blk == 0, 0, jnp.where(blk == nb - 1, 2, 1))

    def bias_spec(sub):
        return pl.BlockSpec((1, hps, tq, win), lambda b, p, i: (variant(i * nsub + sub), p, 0, 0))

    return pl.pallas_call(
        functools.partial(_na_kernel, rows=rows),
        grid=(B, ngroup, nb // nsub),
        in_specs=[
            pl.BlockSpec((1, nsub * tq, width), lambda b, p, i: (b, i, p)),
            pl.BlockSpec((1, S, width), lambda b, p, i: (b, 0, ngroup + p)),
            pl.BlockSpec((1, S, width), lambda b, p, i: (b, 0, 2 * ngroup + p)),
            *[bias_spec(sub) for sub in range(nsub)],
        ],
        out_specs=pl.BlockSpec((1, nsub * tq, width), lambda b, p, i: (b, i, p)),
        out_shape=jax.ShapeDtypeStruct((B, S, NA_WIDTH), BF16),
        compiler_params=pltpu.CompilerParams(
            dimension_semantics=("arbitrary", "arbitrary", "arbitrary"),
            vmem_limit_bytes=VMEM_LIMIT),
        name="na",
    )(qv, qv, qv, *([bias] * nsub)).reshape(B * S, NA_WIDTH)


def _merge_kernel(x_ref, yaT_ref, o1_ref, o2_ref, o3_ref, l1_ref, l2_ref, l3_ref, yc_ref, g_ref,
                  wpa_ref, wpb_ref, wpc_ref, wo_ref, out_ref, o2n_ref, o3n_ref, l2n_ref, l3n_ref):
    a = _dot_tn(yaT_ref[0], wpa_ref[...])
    def token_order(src, dst):
        d, n = src.shape[0], src.shape[1]
        for r in range(d):
            for cb in range(DIL_GROUP_WIDTH // LANE):
                dst.at[cb][pl.ds(r, n, stride=d), :] = (
                    src[r, :, cb * LANE:(cb + 1) * LANE].astype(F32))
        return jnp.concatenate([dst[cb] for cb in range(DIL_GROUP_WIDTH // LANE)], axis=1)

    o2, l2 = token_order(o2_ref, o2n_ref), token_order(l2_ref, l2n_ref)
    o3, l3 = token_order(o3_ref, o3n_ref), token_order(l3_ref, l3n_ref)
    l1 = l1_ref[...]
    mx = jnp.maximum(jnp.maximum(l1, l2), l3)
    w1, w2, w3 = jnp.exp2(l1 - mx), jnp.exp2(l2 - mx), jnp.exp2(l3 - mx)
    yb = (w1 * o1_ref[...].astype(F32) + w2 * o2 + w3 * o3) / (w1 + w2 + w3)
    bm = _dot(yb.astype(BF16), wpb_ref[...])
    cm = _dot(yc_ref[...], wpc_ref[...])
    D = D_MODEL
    merged = (g_ref[:, 0:D].astype(F32) * a + g_ref[:, D:2 * D].astype(F32) * bm
              + g_ref[:, 2 * D:3 * D].astype(F32) * cm)
    out_ref[...] = x_ref[...] + _dot(merged.astype(BF16), wo_ref[...])


def _merge(x2, yaT, obs, lses, yc, gates, wpa, wpb, wpc, wo, B, S, tm):
    T = B * S
    nS = S // tm
    HV = MLA_HEADS * MLA_V_DIM
    GW = DIL_GROUP_WIDTH
    row = lambda w: pl.BlockSpec((tm, w), lambda i: (i, 0))

    def grouped(d):
        if d == 1:
            return pl.BlockSpec((None, None, tm, GW), lambda i: (i // nS, 0, i % nS, 0))
        return pl.BlockSpec((None, d, tm // d, GW), lambda i: (i // nS, 0, i % nS, 0))

    group_specs = [grouped(d) for _, d in DIL_PATTERNS]
    return pl.pallas_call(
        _merge_kernel,
        grid=(T // tm,),
        in_specs=[
            row(D_MODEL),
            pl.BlockSpec((1, HV, tm), lambda i: (i // nS, 0, i % nS)),
            *group_specs, *group_specs,
            row(NA_WIDTH), row(N_BRANCH * D_MODEL),
            _resident(None), _resident(None), _resident(None), _resident(None),
        ],
        out_specs=row(D_MODEL),
        out_shape=jax.ShapeDtypeStruct((T, D_MODEL), F32),
        scratch_shapes=[pltpu.VMEM((GW // LANE, tm, LANE), F32)] * 4,
        compiler_params=pltpu.CompilerParams(
            dimension_semantics=("arbitrary",), vmem_limit_bytes=VMEM_LIMIT),
        name="merge",
    )(x2, yaT, *obs, *lses, yc, gates, wpa, wpb, wpc, wo)


def _ffn_kernel(x_ref, g_ref, w1_ref, w3_ref, w2_ref, gf_ref, out_ref, *, final):
    x = x_ref[...]
    h = _rms(x, g_ref[...]).astype(BF16)
    u = _dot(h, w1_ref[...])
    v = _dot(h, w3_ref[...])
    a = (u / (1.0 + jnp.exp(-u)) * v).astype(BF16)
    y = x + _dot(a, w2_ref[...])
    if final:
        y = _rms(y, gf_ref[...])
    out_ref[...] = y


def _ffn(x2, g_ffn, w1, w3, w2, g_final, tm, final):
    T = x2.shape[0]
    row = pl.BlockSpec((tm, D_MODEL), lambda i: (i, 0))
    return pl.pallas_call(
        functools.partial(_ffn_kernel, final=final),
        grid=(T // tm,),
        in_specs=[row, _resident(None), _resident(None), _resident(None), _resident(None),
                  _resident(None)],
        out_specs=row,
        out_shape=jax.ShapeDtypeStruct((T, D_MODEL), F32),
        compiler_params=pltpu.CompilerParams(
            dimension_semantics=("arbitrary",), vmem_limit_bytes=VMEM_LIMIT),
        name="ffn",
    )(x2, g_ffn, w1, w3, w2, g_final)


def _rot_half_cols(w, dim):
    k = w.shape[0]
    w3 = w.reshape(k, -1, dim)
    return jnp.concatenate([-w3[..., dim // 2:], w3[..., :dim // 2]], axis=-1).reshape(k, -1)


def _pack_layer(w_in, w_uq, w_ukv):
    c_q, c_kv, k_r, qkv_b, qkv_c, gate = jnp.split(
        w_in,
        (MLA_Q_RANK, MLA_Q_RANK + MLA_KV_RANK, MLA_Q_RANK + MLA_KV_RANK + MLA_ROPE_DIM,
         MLA_Q_RANK + MLA_KV_RANK + MLA_ROPE_DIM + 3 * DIL_WIDTH,
         MLA_Q_RANK + MLA_KV_RANK + MLA_ROPE_DIM + 3 * DIL_WIDTH + 3 * NA_WIDTH), axis=1)
    q_b, k_b, v_b = jnp.split(qkv_b, 3, axis=1)
    q_c, k_c, v_c = jnp.split(qkv_c, 3, axis=1)
    q_c = q_c * (HEAD_DIM ** -0.5 * LOG2E)
    pad = jnp.zeros((D_MODEL, COL_QB - (MLA_Q_RANK + MLA_KV_RANK + 2 * MLA_ROPE_DIM)), F32)
    wp = jnp.concatenate(
        [c_q, c_kv, k_r, _rot_half_cols(k_r, MLA_ROPE_DIM), pad,
         q_b, k_b, v_b,
         q_c, k_c, v_c, gate], axis=1).astype(BF16)
    assert wp.shape[1] == COL_END

    uq = w_uq.reshape(MLA_Q_RANK, MLA_HEADS, MLA_NOPE_DIM + MLA_ROPE_DIM)
    zpad = jnp.zeros((MLA_Q_RANK, MLA_HEADS, MLA_HEAD_PAD - MLA_NOPE_DIM - MLA_ROPE_DIM), F32)
    q_main = jnp.concatenate([uq, zpad], axis=-1)
    rope = uq[..., MLA_NOPE_DIM:]
    rope_rot = jnp.concatenate([-rope[..., MLA_ROPE_DIM // 2:], rope[..., :MLA_ROPE_DIM // 2]], -1)
    q_rot = jnp.concatenate([jnp.zeros_like(uq[..., :MLA_NOPE_DIM]), rope_rot, zpad], axis=-1)
    HP = MLA_HEADS * MLA_HEAD_PAD
    wqT = q_main.reshape(MLA_Q_RANK, HP).T.astype(BF16)
    wqrT = q_rot.reshape(MLA_Q_RANK, HP).T.astype(BF16)

    ukv = w_ukv.reshape(MLA_KV_RANK, MLA_HEADS, MLA_NOPE_DIM + MLA_V_DIM)
    k_nope = jnp.concatenate(
        [ukv[..., :MLA_NOPE_DIM],
         jnp.zeros((MLA_KV_RANK, MLA_HEADS, MLA_HEAD_PAD - MLA_NOPE_DIM), F32)], axis=-1)
    place = np.zeros((LANE, MLA_HEADS, MLA_HEAD_PAD), np.float32)
    for j in range(2 * MLA_ROPE_DIM):
        place[j, :, MLA_NOPE_DIM + j % MLA_ROPE_DIM] = 1.0
    wk = jnp.concatenate([k_nope.reshape(MLA_KV_RANK, HP), jnp.asarray(place).reshape(LANE, HP)],
                         axis=0).astype(BF16)
    wvT = ukv[..., MLA_NOPE_DIM:].reshape(MLA_KV_RANK, MLA_HEADS * MLA_V_DIM).T.astype(BF16)
    return wp, wqT, wqrT, wk, wvT


def _rope_cos_sin(seq_len, dim):
    pos = jnp.arange(seq_len, dtype=F32)
    inv = jnp.power(ROPE_THETA, -jnp.arange(0, dim, 2, dtype=F32) / dim)
    ang = pos[:, None] * inv[None, :]
    return jnp.cos(ang), jnp.sin(ang)


def _tables(S):
    cos_a, sin_a = _rope_cos_sin(S, MLA_ROPE_DIM)
    qs = (MLA_NOPE_DIM + MLA_ROPE_DIM) ** -0.5 * LOG2E
    ones = jnp.ones((S, MLA_NOPE_DIM), F32)
    zeros_n = jnp.zeros((S, MLA_NOPE_DIM), F32)
    zpad = jnp.zeros((S, MLA_HEAD_PAD - MLA_NOPE_DIM - MLA_ROPE_DIM), F32)
    cqT = (jnp.concatenate([ones, cos_a, cos_a, zpad], axis=1) * qs).T
    sqT = (jnp.concatenate([zeros_n, sin_a, sin_a, zpad], axis=1) * qs).T
    tk = jnp.concatenate([cos_a, cos_a, sin_a, sin_a, jnp.zeros((S, LANE - 2 * MLA_ROPE_DIM), F32)],
                         axis=1)
    cos_b, sin_b = _rope_cos_sin(S, HEAD_DIM)
    cb = jnp.tile(cos_b, (1, LANE // (HEAD_DIM // 2)))
    sb = jnp.tile(jnp.concatenate([-sin_b, sin_b], axis=1), (1, LANE // HEAD_DIM))
    bs = HEAD_DIM ** -0.5 * LOG2E
    return cqT, sqT, tk, cb * bs, sb * bs, cb, sb


def kernel(x, w_in, g_mix, g_q, g_kv, w_uq, w_ukv, rpb, w_pa, w_pb, w_pc, w_o, g_ffn, w1, w3, w2,
           g_final):
    B, S, D = x.shape
    depth = w_in.shape[0]
    T = B * S
    tm_proj = 512
    tm_post = 512
    tabs = _tables(S)
    x2 = x.reshape(T, D)
    for l in range(depth):
        wp, wqT, wqrT, wk, wvT = _pack_layer(w_in[l], w_uq[l], w_ukv[l])
        qT, k, vT, qkvb0, qkvb1, qkvb2, qkvc, gates = _in_proj(
            x2, g_mix[l][None], wp, g_q[l][None], g_kv[l][None], wqT, wqrT, wk, wvT, tabs,
            B, S, tm_proj)
        yaT = _mla(qT, k.reshape(B, S, -1), vT, B, S, tq=1024, tk=256, tiles_per_step=2)
        obs, lses = [], []
        for grp, (qkvb, (window, dilation)) in enumerate(zip((qkvb0, qkvb1, qkvb2), DIL_PATTERNS)):
            o, lse = _dilated_group(qkvb, B, S, grp, dilation, window // (2 * dilation))
            obs.append(o)
            lses.append(lse)
        yc = _na(qkvc, _na_bias(rpb[l], S // GRID_W), B, S)
        x2 = _merge(x2, yaT, obs, lses, yc, gates, w_pa[l].astype(BF16), w_pb[l].astype(BF16),
                    w_pc[l].astype(BF16), w_o[l].astype(BF16), B, S, tm_post)
        x2 = _ffn(x2, g_ffn[l][None], w1[l].astype(BF16), w3[l].astype(BF16), w2[l].astype(BF16),
                  g_final[None], tm_post, final=(l == depth - 1))
    return x2.reshape(B, S, D)
```
